```python
import jax, jax.numpy as jnp
from jax import lax
import numpy as np

D_MODEL = 4096
BATCH = 4
SEQ = 2048
DEPTH = 1

HEAD_DIM = 128
N_FOX_HEADS = D_MODEL // HEAD_DIM // 2
N_SB_HEADS = D_MODEL // HEAD_DIM // 2
N_MIX_HEADS = N_FOX_HEADS + N_SB_HEADS
FOX_W = N_FOX_HEADS * HEAD_DIM
SB_W = N_SB_HEADS * HEAD_DIM
MIX_W = FOX_W + SB_W
N_IN = 4 * FOX_W + N_FOX_HEADS + 3 * SB_W
BLOCK_Q = 128
N_MEM = 256
XA_HEADS = 4
XA_W = XA_HEADS * HEAD_DIM
N_EXPERTS = 32
TOP_K = 4
D_EXPERT = D_MODEL // 4
SWIGLU_LIMIT = 7.0
SWIGLU_ALPHA = 1.702
LN_EPS = 1e-5
RMS_EPS = 1e-6
DEEPNORM_ALPHA = (2 * DEPTH) ** 0.25
DEEPNORM_BETA = (8 * DEPTH) ** -0.25

kernel_name = 'hybrid_fox_stickbreak_moe_block'


def layer_norm(x, g, b):
    xf = x.astype(jnp.float32)
    mu = jnp.mean(xf, axis=-1, keepdims=True)
    var = jnp.mean(jnp.square(xf - mu), axis=-1, keepdims=True)
    return ((xf - mu) * lax.rsqrt(var + LN_EPS)).astype(x.dtype) * g + b


def rms_norm(x, g):
    xf = x.astype(jnp.float32)
    y = xf * lax.rsqrt(jnp.mean(jnp.square(xf), axis=-1, keepdims=True) + RMS_EPS)
    return y.astype(x.dtype) * g


def forgetting_attention(q, k, v, c):
    S = q.shape[2]
    scale = HEAD_DIM ** -0.5
    outs = []
    for i in range(S // BLOCK_Q):
        q0 = i * BLOCK_Q
        q1 = q0 + BLOCK_Q
        s = jnp.einsum('bhqd,bhkd->bhqk', q[:, :, q0:q1], k[:, :, :q1]).astype(jnp.float32) * scale
        s = s + c[:, :, q0:q1, None] - c[:, :, None, :q1]
        qpos = q0 + jnp.arange(BLOCK_Q)
        kpos = jnp.arange(q1)
        s = jnp.where(kpos[None, :] <= qpos[:, None], s, -jnp.inf)
        p = jax.nn.softmax(s, axis=-1).astype(v.dtype)
        outs.append(jnp.einsum('bhqk,bhkd->bhqd', p, v[:, :, :q1]))
    return jnp.concatenate(outs, axis=2)


def stick_breaking_attention(q, k, v):
    S = q.shape[2]
    scale = HEAD_DIM ** -0.5
    outs = []
    for i in range(S // BLOCK_Q):
        q0 = i * BLOCK_Q
        q1 = q0 + BLOCK_Q
        z = jnp.einsum('bhqd,bhkd->bhqk', q[:, :, q0:q1], k[:, :, :q1]).astype(jnp.float32) * scale
        qpos = q0 + jnp.arange(BLOCK_Q)
        kpos = jnp.arange(q1)
        strict = kpos[None, :] < qpos[:, None]
        log_1m = jnp.where(strict, jax.nn.log_sigmoid(-z), 0.0)
        tail = lax.cumsum(log_1m, axis=3, reverse=True) - log_1m
        a = jnp.where(strict, jnp.exp(jax.nn.log_sigmoid(z) + tail), 0.0)
        outs.append(jnp.einsum('bhqk,bhkd->bhqd', a.astype(v.dtype), v[:, :, :q1]))
    return jnp.concatenate(outs, axis=2)


def hybrid_mixer(h, w_in, b_f, fox_q_norm_g, fox_k_norm_g, mix_norm_g, w_out):
    B, S, _ = h.shape
    p = h @ w_in

    def heads(lo, n):
        return p[..., lo:lo + n * HEAD_DIM].reshape(B, S, n, HEAD_DIM)

    q_f = heads(0, N_FOX_HEADS)
    k_f = heads(FOX_W, N_FOX_HEADS)
    v_f = heads(2 * FOX_W, N_FOX_HEADS)
    g_f = heads(3 * FOX_W, N_FOX_HEADS)
    f_logit = p[..., 4 * FOX_W:4 * FOX_W + N_FOX_HEADS]
    sb0 = 4 * FOX_W + N_FOX_HEADS
    q_s = heads(sb0, N_SB_HEADS)
    k_s = heads(sb0 + SB_W, N_SB_HEADS)
    v_s = heads(sb0 + 2 * SB_W, N_SB_HEADS)

    q_f = rms_norm(q_f, fox_q_norm_g)
    k_f = rms_norm(k_f, fox_k_norm_g)
    log_f = jax.nn.log_sigmoid(f_logit.astype(jnp.float32) + b_f)
    c = jnp.cumsum(log_f, axis=1).transpose(0, 2, 1)

    def bhsd(t):
        return t.transpose(0, 2, 1, 3)

    o_f = bhsd(forgetting_attention(bhsd(q_f), bhsd(k_f), bhsd(v_f), c))
    o_s = bhsd(stick_breaking_attention(bhsd(q_s), bhsd(k_s), bhsd(v_s)))
    o_f = rms_norm(o_f, mix_norm_g[:N_FOX_HEADS]) * jax.nn.sigmoid(g_f)
    o_s = rms_norm(o_s, mix_norm_g[N_FOX_HEADS:])
    o = jnp.concatenate([o_f, o_s], axis=2).reshape(B, S, MIX_W)
    return o @ w_out


def memory_cross_attention(h, mem_n, wq, wkv, wo):
    B, S, _ = h.shape
    M = mem_n.shape[1]
    q = (h @ wq).reshape(B, S, XA_HEADS, HEAD_DIM)
    kv = mem_n @ wkv
    k = kv[..., :XA_W].reshape(B, M, XA_HEADS, HEAD_DIM)
    v = kv[..., XA_W:].reshape(B, M, XA_HEADS, HEAD_DIM)
    s = jnp.einsum('bshd,bmhd->bhsm', q, k).astype(jnp.float32) * HEAD_DIM ** -0.5
    pr = jax.nn.softmax(s, axis=-1).astype(v.dtype)
    o = jnp.einsum('bhsm,bmhd->bshd', pr, v).reshape(B, S, XA_W)
    return o @ wo


def clamped_swiglu_moe(h, router_w, router_b, w_up, b_up, w_down, b_down):
    B, S, D = h.shape
    T = B * S
    ht = h.reshape(T, D)
    logits = (ht @ router_w + router_b).astype(jnp.float32)
    top_v, top_i = lax.top_k(logits, TOP_K)
    top_w = jax.nn.softmax(top_v, axis=-1)
    gate = jnp.sum(jax.nn.one_hot(top_i, N_EXPERTS, dtype=jnp.float32) * top_w[..., None],
                   axis=1).astype(h.dtype)
    out = jnp.zeros_like(ht)
    for e in range(N_EXPERTS):
        gu = ht @ w_up[e] + b_up[e]
        g = jnp.minimum(gu[:, :D_EXPERT], SWIGLU_LIMIT)
        u = jnp.clip(gu[:, D_EXPERT:], -SWIGLU_LIMIT, SWIGLU_LIMIT)
        a = (u + 1.0) * g * jax.nn.sigmoid(SWIGLU_ALPHA * g)
        out = out + gate[:, e:e + 1] * (a @ w_down[e] + b_down[e])
    return out.reshape(B, S, D)


def setup_inputs(seed: int = 0) -> dict:
    key = jax.random.key(seed)
    keys = jax.random.split(key, 48)
    counter = iter(range(48))
    L = DEPTH
    beta = DEEPNORM_BETA
    s_d = D_MODEL ** -0.5

    def normal(shape, scale):
        return jax.random.normal(keys[next(counter)], shape, jnp.float32) * scale

    def gain(shape):
        return 1.0 + normal(shape, 0.02)

    x = normal((BATCH, SEQ, D_MODEL), 1.0)
    mem = normal((BATCH, N_MEM, D_MODEL), 1.0)
    ln_in_g = gain((D_MODEL,))
    ln_in_b = normal((D_MODEL,), 0.02)
    w_in = jnp.concatenate([
        normal((L, D_MODEL, 2 * FOX_W), s_d),
        normal((L, D_MODEL, FOX_W), s_d * beta),
        normal((L, D_MODEL, FOX_W), s_d),
        normal((L, D_MODEL, N_FOX_HEADS), s_d),
        normal((L, D_MODEL, 2 * SB_W), s_d),
        normal((L, D_MODEL, SB_W), s_d * beta),
    ], axis=-1)
    b_f = jnp.linspace(1.0, 6.0, N_FOX_HEADS, dtype=jnp.float32)[None, :] + normal((L, N_FOX_HEADS), 0.1)
    fox_q_norm_g = gain((L, HEAD_DIM))
    fox_k_norm_g = gain((L, HEAD_DIM))
    mix_norm_g = gain((L, N_MIX_HEADS, HEAD_DIM))
    w_out = normal((L, MIX_W, D_MODEL), MIX_W ** -0.5 * beta)
    ln_mix_g = gain((L, D_MODEL))
    ln_mix_b = normal((L, D_MODEL), 0.02)
    mem_ln_g = gain((L, D_MODEL))
    mem_ln_b = normal((L, D_MODEL), 0.02)
    xa_wq = normal((L, D_MODEL, XA_W), s_d)
    xa_wkv = jnp.concatenate([normal((L, D_MODEL, XA_W), s_d),
                              normal((L, D_MODEL, XA_W), s_d * beta)], axis=-1)
    xa_wo = normal((L, XA_W, D_MODEL), XA_W ** -0.5 * beta)
    ln_xa_g = gain((L, D_MODEL))
    ln_xa_b = normal((L, D_MODEL), 0.02)
    router_w = normal((L, D_MODEL, N_EXPERTS), s_d)
    router_b = normal((L, N_EXPERTS), 0.01)
    w_up = normal((L, N_EXPERTS, D_MODEL, 2 * D_EXPERT), s_d)
    b_up = normal((L, N_EXPERTS, 2 * D_EXPERT), 0.01)
    w_down = normal((L, N_EXPERTS, D_EXPERT, D_MODEL), D_EXPERT ** -0.5 * beta)
    b_down = normal((L, N_EXPERTS, D_MODEL), 0.01)
    ln_moe_g = gain((L, D_MODEL))
    ln_moe_b = normal((L, D_MODEL), 0.02)
    return {'x': x, 'mem': mem, 'ln_in_g': ln_in_g, 'ln_in_b': ln_in_b,
            'w_in': w_in, 'b_f': b_f, 'fox_q_norm_g': fox_q_norm_g, 'fox_k_norm_g': fox_k_norm_g,
            'mix_norm_g': mix_norm_g, 'w_out': w_out, 'ln_mix_g': ln_mix_g, 'ln_mix_b': ln_mix_b,
            'mem_ln_g': mem_ln_g, 'mem_ln_b': mem_ln_b, 'xa_wq': xa_wq, 'xa_wkv': xa_wkv,
            'xa_wo': xa_wo, 'ln_xa_g': ln_xa_g, 'ln_xa_b': ln_xa_b,
            'router_w': router_w, 'router_b': router_b, 'w_up': w_up, 'b_up': b_up,
            'w_down': w_down, 'b_down': b_down, 'ln_moe_g': ln_moe_g, 'ln_moe_b': ln_moe_b}


def reference(x, mem, ln_in_g, ln_in_b, w_in, b_f, fox_q_norm_g, fox_k_norm_g, mix_norm_g,
              w_out, ln_mix_g, ln_mix_b, mem_ln_g, mem_ln_b, xa_wq, xa_wkv, xa_wo,
              ln_xa_g, ln_xa_b, router_w, router_b, w_up, b_up, w_down, b_down,
              ln_moe_g, ln_moe_b):
    h = layer_norm(x, ln_in_g, ln_in_b)
    for l in range(DEPTH):
        mix = hybrid_mixer(h, w_in[l], b_f[l], fox_q_norm_g[l], fox_k_norm_g[l], mix_norm_g[l], w_out[l])
        h = layer_norm(DEEPNORM_ALPHA * h + mix, ln_mix_g[l], ln_mix_b[l])
        mem_n = layer_norm(mem, mem_ln_g[l], mem_ln_b[l])
        xa = memory_cross_attention(h, mem_n, xa_wq[l], xa_wkv[l], xa_wo[l])
        h = layer_norm(DEEPNORM_ALPHA * h + xa, ln_xa_g[l], ln_xa_b[l])
        ff = clamped_swiglu_moe(h, router_w[l], router_b[l], w_up[l], b_up[l], w_down[l], b_down[l])
        h = layer_norm(DEEPNORM_ALPHA * h + ff, ln_moe_g[l], ln_moe_b[l])
    return h
```

```python
import functools

import jax
import jax.numpy as jnp
from jax import lax
from jax.experimental import pallas as pl
from jax.experimental.pallas import tpu as pltpu

F32 = jnp.float32
BF16 = jnp.bfloat16

D_MODEL = 4096
HEAD_DIM = 128
N_FOX_HEADS = 16
N_SB_HEADS = 16
FOX_W = N_FOX_HEADS * HEAD_DIM
SB_W = N_SB_HEADS * HEAD_DIM
XA_HEADS = 4
XA_W = XA_HEADS * HEAD_DIM
N_EXPERTS = 32
TOP_K = 4
D_EXPERT = D_MODEL // 4
SWIGLU_LIMIT = 7.0
SWIGLU_ALPHA = 1.702
LN_EPS = 1e-5
RMS_EPS = 1e-6
DEPTH = 1
DEEPNORM_ALPHA = (2 * DEPTH) ** 0.25
ATTN_SCALE = HEAD_DIM ** -0.5

LANES = 128
V7X_VMEM_BYTES = 64 * 1024 * 1024
VMEM_LIMIT = V7X_VMEM_BYTES - 8 * 1024 * 1024

LN_ROWS = 256
MM_ROWS = 2048
MM_SUB = 512
MM_COLS = 256
ATT_BLK = 256
XA_ROWS = 512
RT_ROWS = 256
MOE_ROWS = 256
UP_COLS = 512
DOWN_COLS = 1024
DISPATCH_ROWS = 512
COMBINE_ROWS = 128
DMA_UNROLL = 8


def _params(n_axes, vmem=VMEM_LIMIT):
    return pltpu.CompilerParams(dimension_semantics=("arbitrary",) * n_axes,
                                vmem_limit_bytes=vmem)


def _layer_norm_rows(xf, g, b):
    mu = jnp.mean(xf, axis=-1, keepdims=True)
    xc = xf - mu
    var = jnp.mean(xc * xc, axis=-1, keepdims=True)
    return xc * lax.rsqrt(var + LN_EPS) * g + b


def _pack_bf16_pairs(y):
    n = y.shape[1] // 2
    bits = lax.bitcast_convert_type(y.astype(BF16).astype(F32), jnp.uint32)
    return (bits[:, :n] >> 16) | (bits[:, n:] & jnp.uint32(0xFFFF0000))


def _unpack_bf16_pairs(p):
    lo = lax.bitcast_convert_type(p << 16, F32).astype(BF16)
    hi = lax.bitcast_convert_type(p & jnp.uint32(0xFFFF0000), F32).astype(BF16)
    return lo, hi


def _ln_kernel(x_ref, g_ref, b_ref, *out_refs, kinds):
    y = _layer_norm_rows(x_ref[...].astype(F32), g_ref[...], b_ref[...])
    for ref, kind in zip(out_refs, kinds):
        if kind == "f32":
            ref[...] = y
        elif kind == "bf16":
            ref[...] = y.astype(BF16)
        else:
            ref[...] = _pack_bf16_pairs(y)


def layer_norm_call(x, g, b, kinds):
    rows, d = x.shape
    shapes = {"f32": ((rows, d), F32), "bf16": ((rows, d), BF16), "packed": ((rows, d // 2), jnp.uint32)}
    out_shape = [jax.ShapeDtypeStruct(*shapes[k]) for k in kinds]
    out_specs = [pl.BlockSpec((LN_ROWS, shapes[k][0][1]), lambda i: (i, 0)) for k in kinds]
    return pl.pallas_call(
        functools.partial(_ln_kernel, kinds=kinds),
        grid=(rows // LN_ROWS,),
        in_specs=[pl.BlockSpec((LN_ROWS, d), lambda i: (i, 0)),
                  pl.BlockSpec((1, d), lambda i: (0, 0)),
                  pl.BlockSpec((1, d), lambda i: (0, 0))],
        out_specs=out_specs,
        out_shape=out_shape,
        compiler_params=_params(1),
        name="layer_norm",
    )(x, g.reshape(1, d), b.reshape(1, d))


def _mm_kernel(*refs, n_pairs, has_res, alpha, sub):
    lhs_refs = refs[:n_pairs]
    w_refs = refs[n_pairs:2 * n_pairs]
    pos = 2 * n_pairs
    res_ref = refs[pos] if has_res else None
    pos += int(has_res)
    out_ref = refs[pos]
    wb_refs = refs[pos + 1:]
    for w_ref, wb_ref in zip(w_refs, wb_refs):
        wb_ref[...] = w_ref[...].astype(BF16)
    rows = out_ref.shape[0]

    def body(r, carry):
        sl = pl.ds(pl.multiple_of(r * sub, sub), sub)
        acc = jnp.dot(lhs_refs[0][sl, :], wb_refs[0][...], preferred_element_type=F32)
        for lhs_ref, wb_ref in zip(lhs_refs[1:], wb_refs[1:]):
            acc += jnp.dot(lhs_ref[sl, :], wb_ref[...], preferred_element_type=F32)
        if has_res:
            acc = acc + alpha * res_ref[sl, :]
        out_ref[sl, :] = acc.astype(out_ref.dtype)
        return carry

    lax.fori_loop(0, rows // sub, body, 0)


def matmul_call(pairs, n_cols, out_dtype, res=None, alpha=1.0, tn=MM_COLS, tm=MM_ROWS, name="matmul"):
    m = pairs[0][0].shape[0]
    tm = min(tm, m)
    sub = min(MM_SUB, tm)
    in_specs, args, scratch = [], [], []
    for lhs, _, _, _ in pairs:
        in_specs.append(pl.BlockSpec((tm, lhs.shape[1]), lambda i, j: (i, 0)))
        args.append(lhs)
    for lhs, w, row0, col0 in pairs:
        k = lhs.shape[1]
        in_specs.append(pl.BlockSpec((k, tn), lambda i, j, rb=row0 // k, cb=col0 // tn: (rb, j + cb)))
        args.append(w)
        scratch.append(pltpu.VMEM((k, tn), BF16))
    if res is not None:
        in_specs.append(pl.BlockSpec((tm, tn), lambda i, j: (i, j)))
        args.append(res)
    return pl.pallas_call(
        functools.partial(_mm_kernel, n_pairs=len(pairs), has_res=res is not None, alpha=alpha, sub=sub),
        grid=(m // tm, n_cols // tn),
        in_specs=in_specs,
        out_specs=pl.BlockSpec((tm, tn), lambda i, j: (i, j)),
        out_shape=jax.ShapeDtypeStruct((m, n_cols), out_dtype),
        scratch_shapes=scratch,
        compiler_params=_params(2),
        name=name,
    )(*args)


def _log_sigmoid(x):
    return jnp.minimum(x, 0.0) - jnp.log1p(jnp.exp(-jnp.abs(x)))


def _split3_bf16(x):
    a = x.astype(BF16)
    r = x - a.astype(F32)
    b = r.astype(BF16)
    c = (r - b.astype(F32)).astype(BF16)
    return a, b, c


def _forget_cumsum_kernel(f_ref, b_ref, c_ref, *, chunk):
    seq = f_ref.shape[0]
    row = lax.broadcasted_iota(jnp.int32, (chunk, chunk), 0)
    col = lax.broadcasted_iota(jnp.int32, (chunk, chunk), 1)
    tri = (col <= row).astype(BF16)
    carry = jnp.zeros((1, f_ref.shape[1]), F32)
    for c0 in range(0, seq, chunk):
        lf = _log_sigmoid(f_ref[c0:c0 + chunk, :] + b_ref[...])
        part = sum(jnp.dot(tri, piece, preferred_element_type=F32) for piece in _split3_bf16(lf))
        cs = part + carry
        c_ref[c0:c0 + chunk, :] = cs
        carry = cs[chunk - 1:chunk, :]


def forget_cumsum_call(f_logit, b_f, batch, seq):
    return pl.pallas_call(
        functools.partial(_forget_cumsum_kernel, chunk=ATT_BLK),
        grid=(batch,),
        in_specs=[pl.BlockSpec((seq, LANES), lambda b: (b, 0)),
                  pl.BlockSpec((1, LANES), lambda b: (0, 0))],
        out_specs=pl.BlockSpec((seq, LANES), lambda b: (b, 0)),
        out_shape=jax.ShapeDtypeStruct((batch * seq, LANES), F32),
        compiler_params=_params(1),
        name="forget_cumsum",
    )(f_logit, b_f)


def _rms_rows(xf, g):
    return xf * lax.rsqrt(jnp.mean(xf * xf, axis=-1, keepdims=True) + RMS_EPS) * g


def _store_transposed(k_f32_fn, kt_ref, seq):
    for c0 in range(0, seq, ATT_BLK):
        kt_ref[:, c0:c0 + ATT_BLK] = k_f32_fn(c0).T.astype(BF16)


def _fox_kernel(q_ref, k_ref, v_ref, gate_ref, ccol_ref, crow_ref, gq_ref, gk_ref, gm_ref,
                o_ref, kt_ref):
    qi = pl.program_id(2)
    seq = k_ref.shape[0]
    blk = q_ref.shape[0]

    @pl.when(qi == 0)
    def _():
        _store_transposed(lambda c0: _rms_rows(k_ref[c0:c0 + blk, :].astype(F32), gk_ref[...]), kt_ref, seq)

    qn = (_rms_rows(q_ref[...].astype(F32), gq_ref[...]) * ATTN_SCALE).astype(BF16)
    c_t = ccol_ref[...]

    def step(j, carry, masked):
        m_prev, l_prev, acc = carry
        ks = pl.ds(pl.multiple_of(j * blk, blk), blk)
        s = jnp.dot(qn, kt_ref[:, ks], preferred_element_type=F32)
        s = s + (c_t - crow_ref[:, ks])
        if masked:
            row = lax.broadcasted_iota(jnp.int32, (blk, blk), 0)
            col = lax.broadcasted_iota(jnp.int32, (blk, blk), 1)
            s = jnp.where(col <= row, s, -jnp.inf)
        m_new = jnp.maximum(m_prev, jnp.max(s, axis=-1, keepdims=True))
        p = jnp.exp(s - m_new)
        scale = jnp.exp(m_prev - m_new)
        l_new = scale * l_prev + jnp.sum(p, axis=-1, keepdims=True)
        acc = scale * acc + jnp.dot(p.astype(BF16), v_ref[ks, :], preferred_element_type=F32)
        return m_new, l_new, acc

    init = (jnp.full((blk, 1), -jnp.inf, F32), jnp.zeros((blk, 1), F32), jnp.zeros((blk, HEAD_DIM), F32))
    carry = lax.fori_loop(0, qi, functools.partial(step, masked=False), init)
    _, l_fin, acc = step(qi, carry, masked=True)
    o = acc / l_fin
    o = _rms_rows(o, gm_ref[...]) * jax.nn.sigmoid(gate_ref[...].astype(F32))
    o_ref[...] = o.astype(o_ref.dtype)


def fox_attention_call(p_fox, c_col, c_row, gq, gk, gm, batch, seq):
    nq = seq // ATT_BLK
    h = N_FOX_HEADS
    return pl.pallas_call(
        _fox_kernel,
        grid=(batch, h, nq),
        in_specs=[
            pl.BlockSpec((ATT_BLK, HEAD_DIM), lambda b, hd, qi: (b * nq + qi, hd)),
            pl.BlockSpec((seq, HEAD_DIM), lambda b, hd, qi: (b, h + hd)),
            pl.BlockSpec((seq, HEAD_DIM), lambda b, hd, qi: (b, 2 * h + hd)),
            pl.BlockSpec((ATT_BLK, HEAD_DIM), lambda b, hd, qi: (b * nq + qi, 3 * h + hd)),
            pl.BlockSpec((None, None, ATT_BLK, 1), lambda b, hd, qi: (b, hd, qi, 0)),
            pl.BlockSpec((None, None, 1, seq), lambda b, hd, qi: (b, hd, 0, 0)),
            pl.BlockSpec((1, HEAD_DIM), lambda b, hd, qi: (0, 0)),
            pl.BlockSpec((1, HEAD_DIM), lambda b, hd, qi: (0, 0)),
            pl.BlockSpec((None, 1, HEAD_DIM), lambda b, hd, qi: (hd, 0, 0)),
        ],
        out_specs=pl.BlockSpec((ATT_BLK, HEAD_DIM), lambda b, hd, qi: (b * nq + qi, hd)),
        out_shape=jax.ShapeDtypeStruct((batch * seq, FOX_W), BF16),
        scratch_shapes=[pltpu.VMEM((HEAD_DIM, seq), BF16)],
        compiler_params=_params(3),
        name="fox_attention",
    )(p_fox, p_fox, p_fox, p_fox, c_col, c_row, gq, gk, gm)


def _sb_kernel(q_ref, k_ref, v_ref, gm_ref, o_ref, kt_ref, upper_ref):
    qi = pl.program_id(2)
    seq = k_ref.shape[0]
    blk = q_ref.shape[0]

    @pl.when(qi == 0)
    def _():
        _store_transposed(lambda c0: k_ref[c0:c0 + blk, :].astype(F32), kt_ref, seq)
        row = lax.broadcasted_iota(jnp.int32, (blk, blk), 0)
        col = lax.broadcasted_iota(jnp.int32, (blk, blk), 1)
        upper_ref[...] = (row > col).astype(BF16)

    qs = (q_ref[...].astype(F32) * ATTN_SCALE).astype(BF16)

    def step(j, carry, masked):
        later, acc = carry
        ks = pl.ds(pl.multiple_of(j * blk, blk), blk)
        z = jnp.dot(qs, kt_ref[:, ks], preferred_element_type=F32)
        softplus = jnp.maximum(z, 0.0) + jnp.log1p(jnp.exp(-jnp.abs(z)))
        log_1m = -softplus
        if masked:
            row = lax.broadcasted_iota(jnp.int32, (blk, blk), 0)
            col = lax.broadcasted_iota(jnp.int32, (blk, blk), 1)
            strict = col < row
            log_1m = jnp.where(strict, log_1m, 0.0)
        hi = log_1m.astype(BF16)
        lo = (log_1m - hi.astype(F32)).astype(BF16)
        tail = (jnp.dot(hi, upper_ref[...], preferred_element_type=F32)
                + jnp.dot(lo, upper_ref[...], preferred_element_type=F32)) + later
        a = jnp.exp((z - softplus) + tail)
        if masked:
            a = jnp.where(strict, a, 0.0)
        acc = acc + jnp.dot(a.astype(BF16), v_ref[ks, :], preferred_element_type=F32)
        later = later + jnp.sum(log_1m, axis=-1, keepdims=True)
        return later, acc

    init = (jnp.zeros((blk, 1), F32), jnp.zeros((blk, HEAD_DIM), F32))
    carry = step(qi, init, masked=True)
    _, acc = lax.fori_loop(0, qi, lambda it, c: step(qi - 1 - it, c, masked=False), carry)
    o_ref[...] = _rms_rows(acc, gm_ref[...]).astype(o_ref.dtype)


def sb_attention_call(p_sb, gm, batch, seq):
    nq = seq // ATT_BLK
    h = N_SB_HEADS
    return pl.pallas_call(
        _sb_kernel,
        grid=(batch, h, nq),
        in_specs=[
            pl.BlockSpec((ATT_BLK, HEAD_DIM), lambda b, hd, qi: (b * nq + qi, hd)),
            pl.BlockSpec((seq, HEAD_DIM), lambda b, hd, qi: (b, h + hd)),
            pl.BlockSpec((seq, HEAD_DIM), lambda b, hd, qi: (b, 2 * h + hd)),
            pl.BlockSpec((None, 1, HEAD_DIM), lambda b, hd, qi: (N_FOX_HEADS + hd, 0, 0)),
        ],
        out_specs=pl.BlockSpec((ATT_BLK, HEAD_DIM), lambda b, hd, qi: (b * nq + qi, hd)),
        out_shape=jax.ShapeDtypeStruct((batch * seq, SB_W), BF16),
        scratch_shapes=[pltpu.VMEM((HEAD_DIM, seq), BF16), pltpu.VMEM((ATT_BLK, ATT_BLK), BF16)],
        compiler_params=_params(3),
        name="sb_attention",
    )(p_sb, p_sb, p_sb, gm)


def _xattn_kernel(q_ref, kv_ref, o_ref):
    for hd in range(XA_HEADS):
        cols = slice(hd * HEAD_DIM, (hd + 1) * HEAD_DIM)
        q = q_ref[:, cols]
        k = kv_ref[:, cols]
        v = kv_ref[:, XA_W + hd * HEAD_DIM:XA_W + (hd + 1) * HEAD_DIM]
        s = lax.dot_general(q, k, (((1,), (1,)), ((), ())), preferred_element_type=F32) * ATTN_SCALE
        p = jnp.exp(s - jnp.max(s, axis=-1, keepdims=True))
        p = p / jnp.sum(p, axis=-1, keepdims=True)
        o_ref[:, cols] = jnp.dot(p.astype(BF16), v, preferred_element_type=F32).astype(o_ref.dtype)


def cross_attention_call(q, kv, seq, n_mem):
    rows = q.shape[0]
    per_batch = seq // XA_ROWS
    return pl.pallas_call(
        _xattn_kernel,
        grid=(rows // XA_ROWS,),
        in_specs=[pl.BlockSpec((XA_ROWS, XA_W), lambda i: (i, 0)),
                  pl.BlockSpec((n_mem, 2 * XA_W), lambda i: (i // per_batch, 0))],
        out_specs=pl.BlockSpec((XA_ROWS, XA_W), lambda i: (i, 0)),
        out_shape=jax.ShapeDtypeStruct((rows, XA_W), BF16),
        compiler_params=_params(1),
        name="cross_attention",
    )(q, kv)


def _router_kernel(h_ref, w_ref, b_ref, idx_ref, gate_ref, rank_ref, count_ref, run_ref):
    step = pl.program_id(0)
    rows = h_ref.shape[0]

    @pl.when(step == 0)
    def _():
        run_ref[...] = jnp.zeros_like(run_ref)

    logits = jnp.dot(h_ref[...], w_ref[...], preferred_element_type=F32,
                     precision=lax.Precision.HIGHEST) + b_ref[...]
    lane = lax.broadcasted_iota(jnp.int32, (rows, LANES), 1).astype(F32)
    work = jnp.where(lane < N_EXPERTS, logits, -jnp.inf)
    picks, values = [], []
    for _ in range(TOP_K):
        best = jnp.max(work, axis=-1, keepdims=True)
        first = jnp.min(jnp.where(work == best, lane, float(LANES)), axis=-1, keepdims=True)
        hit = lane == first
        picks.append((first, hit))
        values.append(best)
        work = jnp.where(hit, -jnp.inf, work)
    expv = [jnp.exp(v - values[0]) for v in values]
    denom = sum(expv)

    chosen = sum(hit.astype(F32) for _, hit in picks)
    r = lax.broadcasted_iota(jnp.int32, (rows, rows), 0)
    c = lax.broadcasted_iota(jnp.int32, (rows, rows), 1)
    before = (c < r).astype(BF16)
    prefix = jnp.dot(before, chosen.astype(BF16), preferred_element_type=F32) + run_ref[...]

    idx_out = jnp.zeros((rows, LANES), jnp.int32)
    gate_out = jnp.zeros((rows, LANES), F32)
    rank_out = jnp.zeros((rows, LANES), jnp.int32)
    for k, (first, hit) in enumerate(picks):
        rank_k = jnp.sum(jnp.where(hit, prefix, 0.0), axis=-1, keepdims=True).astype(jnp.int32)
        idx_out = jnp.where(lane == k, first.astype(jnp.int32), idx_out)
        gate_out = jnp.where(lane == k, expv[k] / denom, gate_out)
        rank_out = jnp.where(lane == k, rank_k, rank_out)
    idx_ref[...] = idx_out
    gate_ref[...] = gate_out
    rank_ref[...] = rank_out
    run_ref[...] += jnp.sum(chosen, axis=0, keepdims=True)
    count_ref[...] = run_ref[...]


def router_call(h, w_pad, b_pad):
    rows, d = h.shape
    tok = pl.BlockSpec((RT_ROWS, LANES), lambda i: (i, 0))
    return pl.pallas_call(
        _router_kernel,
        grid=(rows // RT_ROWS,),
        in_specs=[pl.BlockSpec((RT_ROWS, d), lambda i: (i, 0)),
                  pl.BlockSpec((d, LANES), lambda i: (0, 0)),
                  pl.BlockSpec((1, LANES), lambda i: (0, 0))],
        out_specs=[tok, tok, tok, pl.BlockSpec((1, LANES), lambda i: (0, 0))],
        out_shape=[jax.ShapeDtypeStruct((rows, LANES), jnp.int32),
                   jax.ShapeDtypeStruct((rows, LANES), F32),
                   jax.ShapeDtypeStruct((rows, LANES), jnp.int32),
                   jax.ShapeDtypeStruct((1, LANES), F32)],
        scratch_shapes=[pltpu.VMEM((1, LANES), F32)],
        compiler_params=_params(1),
        name="router",
    )(h, w_pad, b_pad)


def _row_copy(src_hbm, dst_ref, sem, src_row, dst_row):
    return pltpu.make_async_copy(src_hbm.at[pl.ds(src_row, 1), :], dst_ref.at[pl.ds(dst_row, 1), :], sem)


def _dispatch_kernel(idx_ref, src_hbm, out_ref, sem):
    rows = out_ref.shape[0]

    def start(r, carry):
        _row_copy(src_hbm, out_ref, sem, idx_ref[0, r], r).start()
        return carry

    def wait(r, carry):
        _row_copy(src_hbm, out_ref, sem, 0, r).wait()
        return carry

    lax.fori_loop(0, rows, start, 0, unroll=DMA_UNROLL)
    lax.fori_loop(0, rows, wait, 0, unroll=DMA_UNROLL)


def dispatch_call(src, row_src):
    p = row_src.shape[0]
    w = src.shape[1]
    steps = p // DISPATCH_ROWS
    return pl.pallas_call(
        _dispatch_kernel,
        grid=(steps,),
        in_specs=[pl.BlockSpec((None, 1, DISPATCH_ROWS), lambda i: (i, 0, 0), memory_space=pltpu.SMEM),
                  pl.BlockSpec(memory_space=pl.ANY)],
        out_specs=pl.BlockSpec((DISPATCH_ROWS, w), lambda i: (i, 0)),
        out_shape=jax.ShapeDtypeStruct((p, w), src.dtype),
        scratch_shapes=[pltpu.SemaphoreType.DMA(())],
        compiler_params=_params(1),
        name="moe_dispatch",
    )(row_src.reshape(steps, 1, DISPATCH_ROWS), src)


def _tile(i, nv_ref):
    return jnp.minimum(i, nv_ref[0] - 1)


def _expert_changed(i, te_ref):
    return jnp.logical_or(i == 0, te_ref[i] != te_ref[jnp.maximum(i - 1, 0)])


def _moe_up_kernel(te_ref, nv_ref, x_ref, wg_ref, wu_ref, bg_ref, bu_ref, a_ref, wgb_ref, wub_ref):
    i = pl.program_id(1)
    half = x_ref.shape[1]

    @pl.when(_expert_changed(i, te_ref))
    def _():
        wgb_ref[...] = wg_ref[...].astype(BF16)
        wub_ref[...] = wu_ref[...].astype(BF16)

    @pl.when(i < nv_ref[0])
    def _():
        lo, hi = _unpack_bf16_pairs(x_ref[...])
        g = (jnp.dot(lo, wgb_ref[:half, :], preferred_element_type=F32)
             + jnp.dot(hi, wgb_ref[half:, :], preferred_element_type=F32)) + bg_ref[...]
        u = (jnp.dot(lo, wub_ref[:half, :], preferred_element_type=F32)
             + jnp.dot(hi, wub_ref[half:, :], preferred_element_type=F32)) + bu_ref[...]
        g = jnp.minimum(g, SWIGLU_LIMIT)
        u = jnp.clip(u, -SWIGLU_LIMIT, SWIGLU_LIMIT)
        a_ref[...] = ((u + 1.0) * g * jax.nn.sigmoid(SWIGLU_ALPHA * g)).astype(a_ref.dtype)

    @pl.when(i >= nv_ref[0])
    def _():
        a_ref[...] = jnp.zeros_like(a_ref)


def moe_up_call(tile_expert, n_valid, xs, w_up, b_up):
    p = xs.shape[0]
    n_tiles = p // MOE_ROWS
    n_col = D_EXPERT // UP_COLS
    d = w_up.shape[1]
    grid_spec = pltpu.PrefetchScalarGridSpec(
        num_scalar_prefetch=2,
        grid=(n_col, n_tiles),
        in_specs=[
            pl.BlockSpec((MOE_ROWS, xs.shape[1]), lambda n, i, te, nv: (_tile(i, nv), 0)),
            pl.BlockSpec((None, d, UP_COLS), lambda n, i, te, nv: (te[_tile(i, nv)], 0, n)),
            pl.BlockSpec((None, d, UP_COLS), lambda n, i, te, nv: (te[_tile(i, nv)], 0, n_col + n)),
            pl.BlockSpec((None, 1, UP_COLS), lambda n, i, te, nv: (te[_tile(i, nv)], 0, n)),
            pl.BlockSpec((None, 1, UP_COLS), lambda n, i, te, nv: (te[_tile(i, nv)], 0, n_col + n)),
        ],
        out_specs=pl.BlockSpec((MOE_ROWS, UP_COLS), lambda n, i, te, nv: (i, n)),
        scratch_shapes=[pltpu.VMEM((d, UP_COLS), BF16), pltpu.VMEM((d, UP_COLS), BF16)],
    )
    return pl.pallas_call(
        _moe_up_kernel,
        grid_spec=grid_spec,
        out_shape=jax.ShapeDtypeStruct((p, D_EXPERT), BF16),
        compiler_params=_params(2),
        name="moe_up",
    )(tile_expert, n_valid, xs, w_up, w_up, b_up, b_up)


def _moe_down_kernel(te_ref, nv_ref, a_ref, w_ref, b_ref, gate_ref, y_ref, wb_ref):
    i = pl.program_id(1)

    @pl.when(_expert_changed(i, te_ref))
    def _():
        wb_ref[...] = w_ref[...].astype(BF16)

    @pl.when(i < nv_ref[0])
    def _():
        y = jnp.dot(a_ref[...], wb_ref[...], preferred_element_type=F32) + b_ref[...]
        y_ref[...] = gate_ref[...] * y

    @pl.when(i >= nv_ref[0])
    def _():
        y_ref[...] = jnp.zeros_like(y_ref)


def moe_down_call(tile_expert, n_valid, act, w_down, b_down, row_gate):
    p = act.shape[0]
    n_tiles = p // MOE_ROWS
    d_out = w_down.shape[2]
    grid_spec = pltpu.PrefetchScalarGridSpec(
        num_scalar_prefetch=2,
        grid=(d_out // DOWN_COLS, n_tiles),
        in_specs=[
            pl.BlockSpec((MOE_ROWS, D_EXPERT), lambda n, i, te, nv: (_tile(i, nv), 0)),
            pl.BlockSpec((None, D_EXPERT, DOWN_COLS), lambda n, i, te, nv: (te[_tile(i, nv)], 0, n)),
            pl.BlockSpec((None, 1, DOWN_COLS), lambda n, i, te, nv: (te[_tile(i, nv)], 0, n)),
            pl.BlockSpec((MOE_ROWS, 1), lambda n, i, te, nv: (_tile(i, nv), 0)),
        ],
        out_specs=pl.BlockSpec((MOE_ROWS, DOWN_COLS), lambda n, i, te, nv: (i, n)),
        scratch_shapes=[pltpu.VMEM((D_EXPERT, DOWN_COLS), BF16)],
    )
    return pl.pallas_call(
        _moe_down_kernel,
        grid_spec=grid_spec,
        out_shape=jax.ShapeDtypeStruct((p, d_out), F32),
        compiler_params=_params(2),
        name="moe_down",
    )(tile_expert, n_valid, act, w_down, b_down, row_gate)


def _combine_kernel(dest_ref, y_hbm, h_ref, g_ref, b_ref, o_ref, buf_ref, sem):
    rows = h_ref.shape[0]

    def start(r, carry):
        for k in range(TOP_K):
            _row_copy(y_hbm, buf_ref.at[k], sem, dest_ref[0, r * TOP_K + k], r).start()
        return carry

    def wait(r, carry):
        for k in range(TOP_K):
            _row_copy(y_hbm, buf_ref.at[k], sem, 0, r).wait()
        return carry

    lax.fori_loop(0, rows, start, 0, unroll=DMA_UNROLL)
    lax.fori_loop(0, rows, wait, 0, unroll=DMA_UNROLL)
    ff = (buf_ref[0] + buf_ref[1]) + (buf_ref[2] + buf_ref[3])
    o_ref[...] = _layer_norm_rows(DEEPNORM_ALPHA * h_ref[...] + ff, g_ref[...], b_ref[...])


def combine_call(dest, y, h, g, b):
    rows, d = h.shape
    steps = rows // COMBINE_ROWS
    return pl.pallas_call(
        _combine_kernel,
        grid=(steps,),
        in_specs=[pl.BlockSpec((None, 1, COMBINE_ROWS * TOP_K), lambda i: (i, 0, 0), memory_space=pltpu.SMEM),
                  pl.BlockSpec(memory_space=pl.ANY),
                  pl.BlockSpec((COMBINE_ROWS, d), lambda i: (i, 0)),
                  pl.BlockSpec((1, d), lambda i: (0, 0)),
                  pl.BlockSpec((1, d), lambda i: (0, 0))],
        out_specs=pl.BlockSpec((COMBINE_ROWS, d), lambda i: (i, 0)),
        out_shape=jax.ShapeDtypeStruct((rows, d), F32),
        scratch_shapes=[pltpu.VMEM((TOP_K, COMBINE_ROWS, d), F32), pltpu.SemaphoreType.DMA(())],
        compiler_params=_params(1),
        name="moe_combine",
    )(dest.reshape(steps, 1, COMBINE_ROWS * TOP_K), y, h, g.reshape(1, d), b.reshape(1, d))


def _pad_lanes(a, rows):
    out = jnp.zeros((rows, LANES), a.dtype)
    return out.at[:, :a.shape[-1]].set(a.reshape(rows, -1))


def kernel(x, mem, ln_in_g, ln_in_b, w_in, b_f, fox_q_norm_g, fox_k_norm_g, mix_norm_g, w_out, ln_mix_g, ln_mix_b, mem_ln_g, mem_ln_b, xa_wq, xa_wkv, xa_wo, ln_xa_g, ln_xa_b, router_w, router_b, w_up, b_up, w_down, b_down, ln_moe_g, ln_moe_b):
    batch, seq, d = x.shape
    n_mem = mem.shape[1]
    tokens = batch * seq
    assert w_in.shape[0] == DEPTH == 1 and d == D_MODEL

    h0, h0b = layer_norm_call(x.reshape(tokens, d), ln_in_g, ln_in_b, ("f32", "bf16"))
    w_in2 = w_in[0]
    sb0 = 4 * FOX_W + N_FOX_HEADS
    p_fox = matmul_call([(h0b, w_in2, 0, 0)], 4 * FOX_W, BF16, name="in_proj_fox")
    f_logit = matmul_call([(h0b, w_in2, 0, 4 * FOX_W)], LANES, F32, tn=LANES, name="in_proj_forget")
    p_sb = matmul_call([(h0b, w_in2[:, sb0:], 0, 0)], 3 * SB_W, BF16, name="in_proj_sb")

    c = forget_cumsum_call(f_logit, _pad_lanes(b_f[0], 1), batch, seq)
    c = c.reshape(batch, seq, LANES)[:, :, :N_FOX_HEADS].transpose(0, 2, 1)
    o_f = fox_attention_call(p_fox, c[..., None], c[:, :, None, :],
                             fox_q_norm_g[0].reshape(1, HEAD_DIM), fox_k_norm_g[0].reshape(1, HEAD_DIM),
                             mix_norm_g[0].reshape(-1, 1, HEAD_DIM), batch, seq)
    o_s = sb_attention_call(p_sb, mix_norm_g[0].reshape(-1, 1, HEAD_DIM), batch, seq)
    r1 = matmul_call([(o_f, w_out[0], 0, 0), (o_s, w_out[0], FOX_W, 0)], d, F32,
                     res=h0, alpha=DEEPNORM_ALPHA, name="out_proj")
    h1, h1b = layer_norm_call(r1, ln_mix_g[0], ln_mix_b[0], ("f32", "bf16"))

    (mem_nb,) = layer_norm_call(mem.reshape(batch * n_mem, d), mem_ln_g[0], mem_ln_b[0], ("bf16",))
    kv = matmul_call([(mem_nb, xa_wkv[0], 0, 0)], 2 * XA_W, BF16, name="xa_kv_proj")
    qx = matmul_call([(h1b, xa_wq[0], 0, 0)], XA_W, BF16, name="xa_q_proj")
    ox = cross_attention_call(qx, kv, seq, n_mem)
    r2 = matmul_call([(ox, xa_wo[0], 0, 0)], d, F32, res=h1, alpha=DEEPNORM_ALPHA, name="xa_out_proj")
    h2, h2p = layer_norm_call(r2, ln_xa_g[0], ln_xa_b[0], ("f32", "packed"))

    idx, gate, rank, counts = router_call(h2, _pad_lanes(router_w[0], d), _pad_lanes(router_b[0], 1))
    idx, gate, rank = idx[:, :TOP_K], gate[:, :TOP_K], rank[:, :TOP_K]
    counts = counts[0, :N_EXPERTS].astype(jnp.int32)
    padded = (counts + MOE_ROWS - 1) // MOE_ROWS * MOE_ROWS
    ends = jnp.cumsum(padded)
    starts = ends - padded
    dest = starts[idx] + rank
    p_rows = tokens * TOP_K + N_EXPERTS * MOE_ROWS
    n_tiles = p_rows // MOE_ROWS
    flat_dest = dest.reshape(-1)
    row_src = jnp.zeros((p_rows,), jnp.int32).at[flat_dest].set(jnp.arange(tokens * TOP_K, dtype=jnp.int32) // TOP_K)
    row_gate = jnp.zeros((p_rows,), F32).at[flat_dest].set(gate.reshape(-1)).reshape(p_rows, 1)
    tile_expert = jnp.minimum(jnp.searchsorted(ends, jnp.arange(n_tiles, dtype=jnp.int32) * MOE_ROWS, side="right"),
                              N_EXPERTS - 1).astype(jnp.int32)
    n_valid = (ends[-1:] // MOE_ROWS).astype(jnp.int32)

    xs = dispatch_call(h2p, row_src)
    act = moe_up_call(tile_expert, n_valid, xs, w_up[0], b_up[0].reshape(N_EXPERTS, 1, -1))
    y = moe_down_call(tile_expert, n_valid, act, w_down[0], b_down[0].reshape(N_EXPERTS, 1, -1), row_gate)
    out = combine_call(flat_dest, y, h2, ln_moe_g[0], ln_moe_b[0])
    return out.reshape(batch, seq, d)
```

```python
import functools

import jax
import jax.numpy as jnp
from jax import lax
from jax.experimental import pallas as pl
from jax.experimental.pallas import tpu as pltpu

F32 = jnp.float32
BF16 = jnp.bfloat16
U32 = jnp.uint32

D_MODEL = 4096
HEAD_DIM = 128
N_FOX_HEADS = 16
N_SB_HEADS = 16
FOX_W = N_FOX_HEADS * HEAD_DIM
SB_W = N_SB_HEADS * HEAD_DIM
XA_HEADS = 4
XA_W = XA_HEADS * HEAD_DIM
N_EXPERTS = 32
TOP_K = 4
D_EXPERT = D_MODEL // 4
SWIGLU_LIMIT = 7.0
SWIGLU_ALPHA = 1.702
LN_EPS = 1e-5
RMS_EPS = 1e-6
DEPTH = 1
DEEPNORM_ALPHA = (2 * DEPTH) ** 0.25
ATTN_SCALE = HEAD_DIM ** -0.5

LANES = 128
BF16_SUBLANES = 16
V7X_VMEM_BYTES = 64 * 1024 * 1024
VMEM_LIMIT = V7X_VMEM_BYTES - 8 * 1024 * 1024

LN_ROWS = 256
MM_ROWS = 2048
MM_SUB = 512
MM_COLS = 256
ATT_ROWS = 512
ATT_KEYS = 256
ATT_HEADS = 4
XA_ROWS = 512
RT_ROWS = 256
MOE_ROWS = 256
UP_COLS = 512
DOWN_COLS = 1024
DISPATCH_ROWS = 512
COMBINE_ROWS = 128
DMA_UNROLL = 8

PACKED_W = D_MODEL // 2
PACKED_TILES = PACKED_W // LANES
Y_TILES = D_MODEL // LANES


def _params(n_axes, vmem=VMEM_LIMIT):
    return pltpu.CompilerParams(dimension_semantics=("arbitrary",) * n_axes,
                                vmem_limit_bytes=vmem)


def _layer_norm_rows(xf, g, b):
    mu = jnp.mean(xf, axis=-1, keepdims=True)
    xc = xf - mu
    var = jnp.mean(xc * xc, axis=-1, keepdims=True)
    return xc * lax.rsqrt(var + LN_EPS) * g + b


def _pack_bf16_pairs(y):
    n = y.shape[1] // 2
    bits = lax.bitcast_convert_type(y.astype(BF16).astype(F32), U32)
    return (bits[:, :n] >> 16) | (bits[:, n:] & jnp.uint32(0xFFFF0000))


def _unpack_bf16_pairs(p):
    lo = lax.bitcast_convert_type(p << 16, F32).astype(BF16)
    hi = lax.bitcast_convert_type(p & jnp.uint32(0xFFFF0000), F32).astype(BF16)
    return lo, hi


def _ln_kernel(x_ref, g_ref, b_ref, *out_refs, kinds):
    y = _layer_norm_rows(x_ref[...].astype(F32), g_ref[...], b_ref[...])
    for ref, kind in zip(out_refs, kinds):
        if kind == "f32":
            ref[...] = y
        elif kind == "bf16":
            ref[...] = y.astype(BF16)
        else:
            ref[...] = _pack_bf16_pairs(y).reshape(ref.shape)


def layer_norm_call(x, g, b, kinds):
    rows, d = x.shape
    out_shape, out_specs = [], []
    for kind in kinds:
        if kind == "packed":
            out_shape.append(jax.ShapeDtypeStruct((rows, PACKED_TILES, LANES), U32))
            out_specs.append(pl.BlockSpec((LN_ROWS, PACKED_TILES, LANES), lambda i: (i, 0, 0)))
        else:
            out_shape.append(jax.ShapeDtypeStruct((rows, d), F32 if kind == "f32" else BF16))
            out_specs.append(pl.BlockSpec((LN_ROWS, d), lambda i: (i, 0)))
    return pl.pallas_call(
        functools.partial(_ln_kernel, kinds=kinds),
        grid=(rows // LN_ROWS,),
        in_specs=[pl.BlockSpec((LN_ROWS, d), lambda i: (i, 0)),
                  pl.BlockSpec((1, d), lambda i: (0, 0)),
                  pl.BlockSpec((1, d), lambda i: (0, 0))],
        out_specs=out_specs,
        out_shape=out_shape,
        compiler_params=_params(1),
        name="layer_norm",
    )(x, g.reshape(1, d), b.reshape(1, d))


_NT_DIMS = (((1,), (1,)), ((), ()))


def _mm_kernel(*refs, transposed, has_res, alpha, sub):
    n_pairs = len(transposed)
    lhs_refs = refs[:n_pairs]
    w_refs = refs[n_pairs:2 * n_pairs]
    pos = 2 * n_pairs
    res_ref = refs[pos] if has_res else None
    pos += int(has_res)
    out_ref = refs[pos]
    wb_refs = refs[pos + 1:]
    for w_ref, wb_ref in zip(w_refs, wb_refs):
        wb_ref[...] = w_ref[...].astype(BF16)
    rows = out_ref.shape[0]

    def body(r, carry):
        sl = pl.ds(pl.multiple_of(r * sub, sub), sub)
        acc = None
        for lhs_ref, wb_ref, tr in zip(lhs_refs, wb_refs, transposed):
            if tr:
                part = lax.dot_general(lhs_ref[sl, :], wb_ref[...], _NT_DIMS, preferred_element_type=F32)
            else:
                part = jnp.dot(lhs_ref[sl, :], wb_ref[...], preferred_element_type=F32)
            acc = part if acc is None else acc + part
        if has_res:
            acc = acc + alpha * res_ref[sl, :]
        out_ref[sl, :] = acc.astype(out_ref.dtype)
        return carry

    lax.fori_loop(0, rows // sub, body, 0)


def matmul_call(pairs, n_cols, out_dtype, res=None, alpha=1.0, tn=MM_COLS, tm=MM_ROWS, name="matmul"):
    m = pairs[0][0].shape[0]
    tm = min(tm, m)
    sub = min(MM_SUB, tm)
    in_specs, args, scratch = [], [], []
    for lhs, _, _, _, _ in pairs:
        in_specs.append(pl.BlockSpec((tm, lhs.shape[1]), lambda i, j: (i, 0)))
        args.append(lhs)
    for lhs, w, k0, col0, tr in pairs:
        k = lhs.shape[1]
        if tr:
            in_specs.append(pl.BlockSpec(
                (pl.Element(tn), pl.Element(k)),
                lambda i, j, k0=k0, col0=col0: (pl.multiple_of(col0 + j * tn, BF16_SUBLANES), k0)))
            scratch.append(pltpu.VMEM((tn, k), BF16))
        else:
            in_specs.append(pl.BlockSpec((k, tn), lambda i, j, rb=k0 // k, cb=col0 // tn: (rb, j + cb)))
            scratch.append(pltpu.VMEM((k, tn), BF16))
        args.append(w)
    if res is not None:
        in_specs.append(pl.BlockSpec((tm, tn), lambda i, j: (i, j)))
        args.append(res)
    return pl.pallas_call(
        functools.partial(_mm_kernel, transposed=tuple(p[4] for p in pairs), has_res=res is not None,
                          alpha=alpha, sub=sub),
        grid=(m // tm, n_cols // tn),
        in_specs=in_specs,
        out_specs=pl.BlockSpec((tm, tn), lambda i, j: (i, j)),
        out_shape=jax.ShapeDtypeStruct((m, n_cols), out_dtype),
        scratch_shapes=scratch,
        compiler_params=_params(2),
        name=name,
    )(*args)


def _log_sigmoid(x):
    return jnp.minimum(x, 0.0) - jnp.log1p(jnp.exp(-jnp.abs(x)))


def _split3_bf16(x):
    a = x.astype(BF16)
    r = x - a.astype(F32)
    b = r.astype(BF16)
    c = (r - b.astype(F32)).astype(BF16)
    return a, b, c


def _forget_cumsum_kernel(f_ref, b_ref, c_ref, *, chunk):
    seq = f_ref.shape[0]
    row = lax.broadcasted_iota(jnp.int32, (chunk, chunk), 0)
    col = lax.broadcasted_iota(jnp.int32, (chunk, chunk), 1)
    tri = (col <= row).astype(BF16)
    carry = jnp.zeros((1, f_ref.shape[1]), F32)
    for c0 in range(0, seq, chunk):
        lf = _log_sigmoid(f_ref[c0:c0 + chunk, :] + b_ref[...])
        part = sum(jnp.dot(tri, piece, preferred_element_type=F32) for piece in _split3_bf16(lf))
        cs = part + carry
        c_ref[c0:c0 + chunk, :] = cs
        carry = cs[chunk - 1:chunk, :]


def forget_cumsum_call(f_logit, b_f, batch, seq):
    return pl.pallas_call(
        functools.partial(_forget_cumsum_kernel, chunk=ATT_KEYS),
        grid=(batch,),
        in_specs=[pl.BlockSpec((seq, LANES), lambda b: (b, 0)),
                  pl.BlockSpec((1, LANES), lambda b: (0, 0))],
        out_specs=pl.BlockSpec((seq, LANES), lambda b: (b, 0)),
        out_shape=jax.ShapeDtypeStruct((batch * seq, LANES), F32),
        compiler_params=_params(1),
        name="forget_cumsum",
    )(f_logit, b_f)


def _rms_rows(xf, g):
    return xf * lax.rsqrt(jnp.mean(xf * xf, axis=-1, keepdims=True) + RMS_EPS) * g


def _head_cols(a):
    return slice(a * HEAD_DIM, (a + 1) * HEAD_DIM)


def _store_values_transposed(v_ref, vt_ref):
    for a in range(ATT_HEADS):
        for s0 in range(0, v_ref.shape[0], ATT_KEYS):
            vt_ref[a, :, s0:s0 + ATT_KEYS] = v_ref[s0:s0 + ATT_KEYS, _head_cols(a)].astype(F32).T.astype(BF16)


def _merge_cols(old, new, col0):
    return new if col0 == 0 else jnp.concatenate([old[:, :col0], new], axis=1)


def _fox_kernel(q_ref, k_ref, v_ref, gate_ref, ckey_ref, cqry_ref, gq_ref, gk_ref, gm_ref,
                o_ref, kn_ref, vt_ref):
    qi = pl.program_id(2)
    seq = k_ref.shape[0]
    nq = q_ref.shape[0]
    blk = ATT_KEYS
    per = nq // blk
    heads = range(ATT_HEADS)

    @pl.when(qi == 0)
    def _():
        _store_values_transposed(v_ref, vt_ref)
        for a in heads:
            for s0 in range(0, seq, blk):
                kn = _rms_rows(k_ref[s0:s0 + blk, _head_cols(a)].astype(F32), gk_ref[...])
                kn_ref[a, s0:s0 + blk, :] = kn.astype(BF16)

    qt = [(_rms_rows(q_ref[:, _head_cols(a)].astype(F32), gq_ref[...]) * ATTN_SCALE).T.astype(BF16)
          for a in heads]
    c_q = [cqry_ref[a] for a in heads]

    def step(a, j, carry, col0, masked):
        m_prev, l_prev, acc = (t[:, col0:] for t in carry)
        ks = pl.ds(pl.multiple_of(j * blk, blk), blk)
        s = jnp.dot(kn_ref[a, ks, :], qt[a][:, col0:], preferred_element_type=F32)
        s = s + (c_q[a][:, col0:] - ckey_ref[a, ks, :])
        if masked:
            key = lax.broadcasted_iota(jnp.int32, s.shape, 0)
            qry = lax.broadcasted_iota(jnp.int32, s.shape, 1)
            s = jnp.where(key <= qry, s, -jnp.inf)
        m_new = jnp.maximum(m_prev, jnp.max(s, axis=0, keepdims=True))
        p = jnp.exp(s - m_new)
        scale = jnp.exp(m_prev - m_new)
        l_new = scale * l_prev + jnp.sum(p, axis=0, keepdims=True)
        acc = scale * acc + jnp.dot(vt_ref[a, :, ks], p.astype(BF16), preferred_element_type=F32)
        return tuple(_merge_cols(old, new, col0) for old, new in zip(carry, (m_new, l_new, acc)))

    init = tuple((jnp.full((1, nq), -jnp.inf, F32), jnp.zeros((1, nq), F32),
                  jnp.zeros((HEAD_DIM, nq), F32)) for _ in heads)
    carries = lax.fori_loop(
        0, per * qi, lambda j, c: tuple(step(a, j, c[a], 0, False) for a in heads), init)
    for r in range(per):
        carries = tuple(step(a, per * qi + r, carries[a], r * blk, True) for a in heads)
    for a in heads:
        _, l_fin, acc = carries[a]
        o = _rms_rows((acc / l_fin).T, gm_ref[a]) * jax.nn.sigmoid(gate_ref[:, _head_cols(a)].astype(F32))
        o_ref[:, _head_cols(a)] = o.astype(o_ref.dtype)


def fox_attention_call(p_fox, c_col, c_row, gq, gk, gm, batch, seq):
    nq = seq // ATT_ROWS
    width = ATT_HEADS * HEAD_DIM
    groups = N_FOX_HEADS // ATT_HEADS
    return pl.pallas_call(
        _fox_kernel,
        grid=(batch, groups, nq),
        in_specs=[
            pl.BlockSpec((ATT_ROWS, width), lambda b, hp, qi: (b * nq + qi, hp)),
            pl.BlockSpec((seq, width), lambda b, hp, qi: (b, groups + hp)),
            pl.BlockSpec((seq, width), lambda b, hp, qi: (b, 2 * groups + hp)),
            pl.BlockSpec((ATT_ROWS, width), lambda b, hp, qi: (b * nq + qi, 3 * groups + hp)),
            pl.BlockSpec((None, ATT_HEADS, seq, 1), lambda b, hp, qi: (b, hp, 0, 0)),
            pl.BlockSpec((None, ATT_HEADS, 1, ATT_ROWS), lambda b, hp, qi: (b, hp, 0, qi)),
            pl.BlockSpec((1, HEAD_DIM), lambda b, hp, qi: (0, 0)),
            pl.BlockSpec((1, HEAD_DIM), lambda b, hp, qi: (0, 0)),
            pl.BlockSpec((ATT_HEADS, 1, HEAD_DIM), lambda b, hp, qi: (hp, 0, 0)),
        ],
        out_specs=pl.BlockSpec((ATT_ROWS, width), lambda b, hp, qi: (b * nq + qi, hp)),
        out_shape=jax.ShapeDtypeStruct((batch * seq, FOX_W), BF16),
        scratch_shapes=[pltpu.VMEM((ATT_HEADS, seq, HEAD_DIM), BF16), pltpu.VMEM((ATT_HEADS, HEAD_DIM, seq), BF16)],
        compiler_params=_params(3),
        name="fox_attention",
    )(p_fox, p_fox, p_fox, p_fox, c_col, c_row, gq, gk, gm)


def _sb_kernel(q_ref, k_ref, v_ref, gm_ref, o_ref, vt_ref, after_ref):
    qi = pl.program_id(2)
    nq = q_ref.shape[0]
    blk = ATT_KEYS
    per = nq // blk
    heads = range(ATT_HEADS)

    @pl.when(qi == 0)
    def _():
        _store_values_transposed(v_ref, vt_ref)
        row = lax.broadcasted_iota(jnp.int32, (blk, blk), 0)
        col = lax.broadcasted_iota(jnp.int32, (blk, blk), 1)
        after_ref[...] = (col > row).astype(BF16)

    qt = [(q_ref[:, _head_cols(a)].astype(F32) * ATTN_SCALE).T.astype(BF16) for a in heads]

    def step(a, j, carry, col0, masked):
        later, acc = (t[:, col0:] for t in carry)
        ks = pl.ds(pl.multiple_of(j * blk, blk), blk)
        z = jnp.dot(k_ref[ks, _head_cols(a)], qt[a][:, col0:], preferred_element_type=F32)
        softplus = jnp.maximum(z, 0.0) + jnp.log(1.0 + jnp.exp(-jnp.abs(z)))
        log_1m = -softplus
        if masked:
            key = lax.broadcasted_iota(jnp.int32, z.shape, 0)
            qry = lax.broadcasted_iota(jnp.int32, z.shape, 1)
            strict = key < qry
            log_1m = jnp.where(strict, log_1m, 0.0)
        hi = log_1m.astype(BF16)
        lo = (log_1m - hi.astype(F32)).astype(BF16)
        tail = (jnp.dot(after_ref[...], hi, preferred_element_type=F32)
                + jnp.dot(after_ref[...], lo, preferred_element_type=F32)) + later
        w = jnp.exp((z - softplus) + tail)
        if masked:
            w = jnp.where(strict, w, 0.0)
        acc = acc + jnp.dot(vt_ref[a, :, ks], w.astype(BF16), preferred_element_type=F32)
        later = later + jnp.sum(log_1m, axis=0, keepdims=True)
        return tuple(_merge_cols(old, new, col0) for old, new in zip(carry, (later, acc)))

    carries = tuple((jnp.zeros((1, nq), F32), jnp.zeros((HEAD_DIM, nq), F32)) for _ in heads)
    for r in reversed(range(per)):
        carries = tuple(step(a, per * qi + r, carries[a], r * blk, True) for a in heads)
    carries = lax.fori_loop(
        0, per * qi, lambda it, c: tuple(step(a, per * qi - 1 - it, c[a], 0, False) for a in heads), carries)
    for a in heads:
        o_ref[:, _head_cols(a)] = _rms_rows(carries[a][1].T, gm_ref[a]).astype(o_ref.dtype)


def sb_attention_call(p_sb, gm, batch, seq):
    nq = seq // ATT_ROWS
    width = ATT_HEADS * HEAD_DIM
    groups = N_SB_HEADS // ATT_HEADS
    return pl.pallas_call(
        _sb_kernel,
        grid=(batch, groups, nq),
        in_specs=[
            pl.BlockSpec((ATT_ROWS, width), lambda b, hp, qi: (b * nq + qi, hp)),
            pl.BlockSpec((seq, width), lambda b, hp, qi: (b, groups + hp)),
            pl.BlockSpec((seq, width), lambda b, hp, qi: (b, 2 * groups + hp)),
            pl.BlockSpec((ATT_HEADS, 1, HEAD_DIM), lambda b, hp, qi: (N_FOX_HEADS // ATT_HEADS + hp, 0, 0)),
        ],
        out_specs=pl.BlockSpec((ATT_ROWS, width), lambda b, hp, qi: (b * nq + qi, hp)),
        out_shape=jax.ShapeDtypeStruct((batch * seq, SB_W), BF16),
        scratch_shapes=[pltpu.VMEM((ATT_HEADS, HEAD_DIM, seq), BF16), pltpu.VMEM((ATT_KEYS, ATT_KEYS), BF16)],
        compiler_params=_params(3),
        name="sb_attention",
    )(p_sb, p_sb, p_sb, gm)


def _xattn_kernel(q_ref, kv_ref, o_ref):
    for hd in range(XA_HEADS):
        cols = _head_cols(hd)
        q = q_ref[:, cols]
        k = kv_ref[:, cols]
        v = kv_ref[:, XA_W + hd * HEAD_DIM:XA_W + (hd + 1) * HEAD_DIM]
        s = lax.dot_general(q, k, _NT_DIMS, preferred_element_type=F32) * ATTN_SCALE
        p = jnp.exp(s - jnp.max(s, axis=-1, keepdims=True))
        p = p / jnp.sum(p, axis=-1, keepdims=True)
        o_ref[:, cols] = jnp.dot(p.astype(BF16), v, preferred_element_type=F32).astype(o_ref.dtype)


def cross_attention_call(q, kv, seq, n_mem):
    rows = q.shape[0]
    per_batch = seq // XA_ROWS
    return pl.pallas_call(
        _xattn_kernel,
        grid=(rows // XA_ROWS,),
        in_specs=[pl.BlockSpec((XA_ROWS, XA_W), lambda i: (i, 0)),
                  pl.BlockSpec((n_mem, 2 * XA_W), lambda i: (i // per_batch, 0))],
        out_specs=pl.BlockSpec((XA_ROWS, XA_W), lambda i: (i, 0)),
        out_shape=jax.ShapeDtypeStruct((rows, XA_W), BF16),
        compiler_params=_params(1),
        name="cross_attention",
    )(q, kv)


def _router_kernel(h_ref, w_ref, b_ref, idx_ref, gate_ref, rank_ref, count_ref, run_ref):
    step = pl.program_id(0)
    rows = h_ref.shape[0]

    @pl.when(step == 0)
    def _():
        run_ref[...] = jnp.zeros_like(run_ref)

    logits = jnp.dot(h_ref[...], w_ref[...], preferred_element_type=F32,
                     precision=lax.Precision.HIGHEST) + b_ref[...]
    lane = lax.broadcasted_iota(jnp.int32, (rows, LANES), 1).astype(F32)
    work = jnp.where(lane < N_EXPERTS, logits, -jnp.inf)
    picks, values = [], []
    for _ in range(TOP_K):
        best = jnp.max(work, axis=-1, keepdims=True)
        first = jnp.min(jnp.where(work == best, lane, float(LANES)), axis=-1, keepdims=True)
        hit = lane == first
        picks.append((first, hit))
        values.append(best)
        work = jnp.where(hit, -jnp.inf, work)
    expv = [jnp.exp(v - values[0]) for v in values]
    denom = sum(expv)

    chosen = sum(hit.astype(F32) for _, hit in picks)
    r = lax.broadcasted_iota(jnp.int32, (rows, rows), 0)
    c = lax.broadcasted_iota(jnp.int32, (rows, rows), 1)
    before = (c < r).astype(BF16)
    prefix = jnp.dot(before, chosen.astype(BF16), preferred_element_type=F32) + run_ref[...]

    idx_out = jnp.zeros((rows, LANES), jnp.int32)
    gate_out = jnp.zeros((rows, LANES), F32)
    rank_out = jnp.zeros((rows, LANES), jnp.int32)
    for k, (first, hit) in enumerate(picks):
        rank_k = jnp.sum(jnp.where(hit, prefix, 0.0), axis=-1, keepdims=True).astype(jnp.int32)
        idx_out = jnp.where(lane == k, first.astype(jnp.int32), idx_out)
        gate_out = jnp.where(lane == k, expv[k] / denom, gate_out)
        rank_out = jnp.where(lane == k, rank_k, rank_out)
    idx_ref[...] = idx_out
    gate_ref[...] = gate_out
    rank_ref[...] = rank_out
    run_ref[...] += jnp.sum(chosen, axis=0, keepdims=True)
    count_ref[...] = run_ref[...]


def router_call(h, w_pad, b_pad):
    rows, d = h.shape
    tok = pl.BlockSpec((RT_ROWS, LANES), lambda i: (i, 0))
    return pl.pallas_call(
        _router_kernel,
        grid=(rows // RT_ROWS,),
        in_specs=[pl.BlockSpec((RT_ROWS, d), lambda i: (i, 0)),
                  pl.BlockSpec((d, LANES), lambda i: (0, 0)),
                  pl.BlockSpec((1, LANES), lambda i: (0, 0))],
        out_specs=[tok, tok, tok, pl.BlockSpec((1, LANES), lambda i: (0, 0))],
        out_shape=[jax.ShapeDtypeStruct((rows, LANES), jnp.int32),
                   jax.ShapeDtypeStruct((rows, LANES), F32),
                   jax.ShapeDtypeStruct((rows, LANES), jnp.int32),
                   jax.ShapeDtypeStruct((1, LANES), F32)],
        scratch_shapes=[pltpu.VMEM((1, LANES), F32)],
        compiler_params=_params(1),
        name="router",
    )(h, w_pad, b_pad)


def _slab_copy(zeros_ref, dst_hbm, sem, row):
    return pltpu.make_async_copy(zeros_ref, dst_hbm.at[pl.ds(row, MOE_ROWS)], sem)


def _token_copy(src_ref, dst_ref, sem, src_row, dst_row):
    return pltpu.make_async_copy(src_ref.at[pl.ds(src_row, 1)], dst_ref.at[pl.ds(dst_row, 1)], sem)


def _dispatch_kernel(fill_ref, dest_ref, h_ref, xs_hbm, zeros_ref, slab_sem, row_sem):
    tokens = h_ref.shape[0]

    @pl.when(pl.program_id(0) == 0)
    def _():
        zeros_ref[...] = jnp.zeros_like(zeros_ref)

        def group_slab(e, carry):
            copy = _slab_copy(zeros_ref, xs_hbm, slab_sem, fill_ref[e])
            copy.start()
            copy.wait()
            return carry

        lax.fori_loop(0, N_EXPERTS, group_slab, 0)
        tail0, n_tail = fill_ref[N_EXPERTS], fill_ref[N_EXPERTS + 1]

        def tail_start(s, carry):
            _slab_copy(zeros_ref, xs_hbm, slab_sem, tail0 + s * MOE_ROWS).start()
            return carry

        def tail_wait(s, carry):
            _slab_copy(zeros_ref, xs_hbm, slab_sem, tail0).wait()
            return carry

        lax.fori_loop(0, n_tail, tail_start, 0)
        lax.fori_loop(0, n_tail, tail_wait, 0)

    def start(t, carry):
        for k in range(TOP_K):
            _token_copy(h_ref, xs_hbm, row_sem, t, dest_ref[0, t * TOP_K + k]).start(priority=k % 2)
        return carry

    def wait(t, carry):
        for k in range(TOP_K):
            _token_copy(h_ref, xs_hbm, row_sem, t, 0).wait()
        return carry

    lax.fori_loop(0, tokens, start, 0, unroll=DMA_UNROLL)
    lax.fori_loop(0, tokens, wait, 0, unroll=DMA_UNROLL)


def dispatch_call(fill, dest, packed, p_rows):
    tokens = packed.shape[0]
    steps = tokens // DISPATCH_ROWS
    slab = packed.shape[1:]
    grid_spec = pltpu.PrefetchScalarGridSpec(
        num_scalar_prefetch=1,
        grid=(steps,),
        in_specs=[pl.BlockSpec((None, 1, DISPATCH_ROWS * TOP_K), lambda i, f: (i, 0, 0), memory_space=pltpu.SMEM),
                  pl.BlockSpec((DISPATCH_ROWS,) + slab, lambda i, f: (i, 0, 0))],
        out_specs=pl.BlockSpec(memory_space=pl.ANY),
        scratch_shapes=[pltpu.VMEM((MOE_ROWS,) + slab, U32),
                        pltpu.SemaphoreType.DMA(()), pltpu.SemaphoreType.DMA(())],
    )
    return pl.pallas_call(
        _dispatch_kernel,
        grid_spec=grid_spec,
        out_shape=jax.ShapeDtypeStruct((p_rows + MOE_ROWS,) + slab, U32),
        compiler_params=_params(1),
        name="moe_dispatch",
    )(fill, dest.reshape(steps, 1, DISPATCH_ROWS * TOP_K), packed)


def _tile(i, nv_ref):
    return jnp.minimum(i, nv_ref[0] - 1)


def _expert_changed(i, te_ref):
    return jnp.logical_or(i == 0, te_ref[i] != te_ref[jnp.maximum(i - 1, 0)])


def _moe_up_kernel(te_ref, nv_ref, x_ref, wg_ref, wu_ref, bg_ref, bu_ref, a_ref, wgb_ref, wub_ref):
    i = pl.program_id(1)
    half = PACKED_W

    @pl.when(_expert_changed(i, te_ref))
    def _():
        wgb_ref[...] = wg_ref[...].astype(BF16)
        wub_ref[...] = wu_ref[...].astype(BF16)

    @pl.when(i < nv_ref[0])
    def _():
        lo, hi = _unpack_bf16_pairs(x_ref[...].reshape(x_ref.shape[0], half))
        g = (jnp.dot(lo, wgb_ref[:half, :], preferred_element_type=F32)
             + jnp.dot(hi, wgb_ref[half:, :], preferred_element_type=F32)) + bg_ref[...]
        u = (jnp.dot(lo, wub_ref[:half, :], preferred_element_type=F32)
             + jnp.dot(hi, wub_ref[half:, :], preferred_element_type=F32)) + bu_ref[...]
        g = jnp.minimum(g, SWIGLU_LIMIT)
        u = jnp.clip(u, -SWIGLU_LIMIT, SWIGLU_LIMIT)
        a_ref[...] = ((u + 1.0) * g * jax.nn.sigmoid(SWIGLU_ALPHA * g)).astype(a_ref.dtype)

    @pl.when(i >= nv_ref[0])
    def _():
        a_ref[...] = jnp.zeros_like(a_ref)


def moe_up_call(tile_expert, n_valid, xs, w_up, b_up, p_rows):
    n_tiles = p_rows // MOE_ROWS
    n_col = D_EXPERT // UP_COLS
    d = w_up.shape[1]
    grid_spec = pltpu.PrefetchScalarGridSpec(
        num_scalar_prefetch=2,
        grid=(n_col, n_tiles),
        in_specs=[
            pl.BlockSpec((MOE_ROWS,) + xs.shape[1:], lambda n, i, te, nv: (_tile(i, nv), 0, 0)),
            pl.BlockSpec((None, d, UP_COLS), lambda n, i, te, nv: (te[_tile(i, nv)], 0, n)),
            pl.BlockSpec((None, d, UP_COLS), lambda n, i, te, nv: (te[_tile(i, nv)], 0, n_col + n)),
            pl.BlockSpec((None, 1, UP_COLS), lambda n, i, te, nv: (te[_tile(i, nv)], 0, n)),
            pl.BlockSpec((None, 1, UP_COLS), lambda n, i, te, nv: (te[_tile(i, nv)], 0, n_col + n)),
        ],
        out_specs=pl.BlockSpec((MOE_ROWS, UP_COLS), lambda n, i, te, nv: (i, n)),
        scratch_shapes=[pltpu.VMEM((d, UP_COLS), BF16), pltpu.VMEM((d, UP_COLS), BF16)],
    )
    return pl.pallas_call(
        _moe_up_kernel,
        grid_spec=grid_spec,
        out_shape=jax.ShapeDtypeStruct((p_rows, D_EXPERT), BF16),
        compiler_params=_params(2),
        name="moe_up",
    )(tile_expert, n_valid, xs, w_up, w_up, b_up, b_up)


def _moe_down_kernel(te_ref, nv_ref, a_ref, w_ref, b_ref, y_ref, wb_ref):
    i = pl.program_id(1)

    @pl.when(_expert_changed(i, te_ref))
    def _():
        wb_ref[...] = w_ref[...].astype(BF16)

    @pl.when(i < nv_ref[0])
    def _():
        y = jnp.dot(a_ref[...], wb_ref[...], preferred_element_type=F32) + b_ref[...]
        y_ref[...] = y.reshape(y_ref.shape)

    @pl.when(i >= nv_ref[0])
    def _():
        y_ref[...] = jnp.zeros_like(y_ref)


def moe_down_call(tile_expert, n_valid, act, w_down, b_down):
    p_rows = act.shape[0]
    n_tiles = p_rows // MOE_ROWS
    d_out = w_down.shape[2]
    grid_spec = pltpu.PrefetchScalarGridSpec(
        num_scalar_prefetch=2,
        grid=(d_out // DOWN_COLS, n_tiles),
        in_specs=[
            pl.BlockSpec((MOE_ROWS, D_EXPERT), lambda n, i, te, nv: (_tile(i, nv), 0)),
            pl.BlockSpec((None, D_EXPERT, DOWN_COLS), lambda n, i, te, nv: (te[_tile(i, nv)], 0, n)),
            pl.BlockSpec((None, 1, DOWN_COLS), lambda n, i, te, nv: (te[_tile(i, nv)], 0, n)),
        ],
        out_specs=pl.BlockSpec((MOE_ROWS, DOWN_COLS // LANES, LANES), lambda n, i, te, nv: (i, n, 0)),
        scratch_shapes=[pltpu.VMEM((D_EXPERT, DOWN_COLS), BF16)],
    )
    return pl.pallas_call(
        _moe_down_kernel,
        grid_spec=grid_spec,
        out_shape=jax.ShapeDtypeStruct((p_rows, d_out // LANES, LANES), F32),
        compiler_params=_params(2),
        name="moe_down",
    )(tile_expert, n_valid, act, w_down, b_down)


def _combine_kernel(dest_ref, next_ref, y_hbm, gate_ref, h_ref, g_ref, b_ref, o_ref, buf_ref, sems):
    step = pl.program_id(0)
    rows = h_ref.shape[0]
    slot = step % 2

    def issue(idx_ref, to_slot):
        def start(r, carry):
            for k in range(TOP_K):
                _token_copy(y_hbm, buf_ref.at[to_slot, k], sems.at[to_slot], idx_ref[0, r * TOP_K + k], r
                            ).start(priority=k % 2)
            return carry
        lax.fori_loop(0, rows, start, 0, unroll=DMA_UNROLL)

    @pl.when(step == 0)
    def _():
        issue(dest_ref, 0)

    @pl.when(step + 1 < pl.num_programs(0))
    def _():
        issue(next_ref, 1 - slot)

    def wait(r, carry):
        for k in range(TOP_K):
            _token_copy(y_hbm, buf_ref.at[slot, k], sems.at[slot], 0, r).wait()
        return carry

    lax.fori_loop(0, rows, wait, 0, unroll=DMA_UNROLL)
    ff = None
    for k in range(TOP_K):
        term = gate_ref[:, k:k + 1] * buf_ref[slot, k].reshape(rows, D_MODEL)
        ff = term if ff is None else ff + term
    o_ref[...] = _layer_norm_rows(DEEPNORM_ALPHA * h_ref[...] + ff, g_ref[...], b_ref[...])


def combine_call(dest, y, gate, h, g, b):
    rows, d = h.shape
    steps = rows // COMBINE_ROWS
    dest3 = dest.reshape(steps, 1, COMBINE_ROWS * TOP_K)
    idx_block = (None, 1, COMBINE_ROWS * TOP_K)
    return pl.pallas_call(
        _combine_kernel,
        grid=(steps,),
        in_specs=[pl.BlockSpec(idx_block, lambda i: (i, 0, 0), memory_space=pltpu.SMEM),
                  pl.BlockSpec(idx_block, lambda i: (jnp.minimum(i + 1, steps - 1), 0, 0), memory_space=pltpu.SMEM),
                  pl.BlockSpec(memory_space=pl.ANY),
                  pl.BlockSpec((COMBINE_ROWS, LANES), lambda i: (i, 0)),
                  pl.BlockSpec((COMBINE_ROWS, d), lambda i: (i, 0)),
                  pl.BlockSpec((1, d), lambda i: (0, 0)),
                  pl.BlockSpec((1, d), lambda i: (0, 0))],
        out_specs=pl.BlockSpec((COMBINE_ROWS, d), lambda i: (i, 0)),
        out_shape=jax.ShapeDtypeStruct((rows, d), F32),
        scratch_shapes=[pltpu.VMEM((2, TOP_K, COMBINE_ROWS, Y_TILES, LANES), F32),
                        pltpu.SemaphoreType.DMA((2,))],
        compiler_params=_params(1),
        name="moe_combine",
    )(dest3, dest3, y, gate, h, g.reshape(1, d), b.reshape(1, d))


def _pad_lanes(a, rows):
    out = jnp.zeros((rows, LANES), a.dtype)
    return out.at[:, :a.shape[-1]].set(a.reshape(rows, -1))


def kernel(x, mem, ln_in_g, ln_in_b, w_in, b_f, fox_q_norm_g, fox_k_norm_g, mix_norm_g, w_out, ln_mix_g, ln_mix_b, mem_ln_g, mem_ln_b, xa_wq, xa_wkv, xa_wo, ln_xa_g, ln_xa_b, router_w, router_b, w_up, b_up, w_down, b_down, ln_moe_g, ln_moe_b):
    batch, seq, d = x.shape
    n_mem = mem.shape[1]
    tokens = batch * seq
    assert w_in.shape[0] == DEPTH == 1 and d == D_MODEL

    h0, h0b = layer_norm_call(x.reshape(tokens, d), ln_in_g, ln_in_b, ("f32", "bf16"))
    w_in_t = jnp.swapaxes(w_in, 1, 2)[0]
    sb0 = 4 * FOX_W + N_FOX_HEADS
    p_fox = matmul_call([(h0b, w_in_t, 0, 0, True)], 4 * FOX_W, BF16, name="in_proj_fox")
    f_logit = matmul_call([(h0b, w_in_t, 0, 4 * FOX_W, True)], LANES, F32, tn=LANES, name="in_proj_forget")
    p_sb = matmul_call([(h0b, w_in_t, 0, sb0, True)], 3 * SB_W, BF16, name="in_proj_sb")

    c = forget_cumsum_call(f_logit, _pad_lanes(b_f[0], 1), batch, seq)
    c = c.reshape(batch, seq, LANES)[:, :, :N_FOX_HEADS].transpose(0, 2, 1)
    gm = mix_norm_g[0].reshape(-1, 1, HEAD_DIM)
    o_f = fox_attention_call(p_fox, c[..., None], c[:, :, None, :],
                             fox_q_norm_g[0].reshape(1, HEAD_DIM), fox_k_norm_g[0].reshape(1, HEAD_DIM),
                             gm, batch, seq)
    o_s = sb_attention_call(p_sb, gm, batch, seq)
    r1 = matmul_call([(o_f, w_out[0], 0, 0, False), (o_s, w_out[0], FOX_W, 0, False)], d, F32,
                     res=h0, alpha=DEEPNORM_ALPHA, name="out_proj")
    h1, h1b = layer_norm_call(r1, ln_mix_g[0], ln_mix_b[0], ("f32", "bf16"))

    (mem_nb,) = layer_norm_call(mem.reshape(batch * n_mem, d), mem_ln_g[0], mem_ln_b[0], ("bf16",))
    kv = matmul_call([(mem_nb, xa_wkv[0], 0, 0, False)], 2 * XA_W, BF16, name="xa_kv_proj")
    qx = matmul_call([(h1b, xa_wq[0], 0, 0, False)], XA_W, BF16, name="xa_q_proj")
    ox = cross_attention_call(qx, kv, seq, n_mem)
    r2 = matmul_call([(ox, xa_wo[0], 0, 0, False)], d, F32, res=h1, alpha=DEEPNORM_ALPHA, name="xa_out_proj")
    h2, h2p = layer_norm_call(r2, ln_xa_g[0], ln_xa_b[0], ("f32", "packed"))

    idx, gate, rank, counts = router_call(h2, _pad_lanes(router_w[0], d), _pad_lanes(router_b[0], 1))
    counts = counts[0, :N_EXPERTS].astype(jnp.int32)
    padded = (counts + MOE_ROWS - 1) // MOE_ROWS * MOE_ROWS
    ends = jnp.cumsum(padded)
    starts = ends - padded
    dest = (starts[idx[:, :TOP_K]] + rank[:, :TOP_K]).reshape(-1)
    p_rows = tokens * TOP_K + N_EXPERTS * MOE_ROWS
    n_tiles = p_rows // MOE_ROWS
    tile_row0 = jnp.arange(n_tiles, dtype=jnp.int32) * MOE_ROWS
    tile_expert = jnp.minimum(jnp.sum(ends[None, :] <= tile_row0[:, None], axis=1), N_EXPERTS - 1).astype(jnp.int32)
    n_valid = (ends[-1:] // MOE_ROWS).astype(jnp.int32)
    fill = jnp.concatenate([starts + counts, ends[-1:], (p_rows + MOE_ROWS - ends[-1:]) // MOE_ROWS]).astype(jnp.int32)

    xs = dispatch_call(fill, dest, h2p, p_rows)
    act = moe_up_call(tile_expert, n_valid, xs, w_up[0], b_up[0].reshape(N_EXPERTS, 1, -1), p_rows)
    y = moe_down_call(tile_expert, n_valid, act, w_down[0], b_down[0].reshape(N_EXPERTS, 1, -1))
    out = combine_call(dest, y, gate, h2, ln_moe_g[0], ln_moe_b[0])
    return out.reshape(batch, seq, d)
```

```python
import functools

import jax
import jax.numpy as jnp
from jax import lax
from jax.experimental import pallas as pl
from jax.experimental.pallas import tpu as pltpu

F32 = jnp.float32
BF16 = jnp.bfloat16

D_MODEL = 4096
HEAD_DIM = 128
N_FOX_HEADS = 16
N_SB_HEADS = 16
FOX_W = N_FOX_HEADS * HEAD_DIM
SB_W = N_SB_HEADS * HEAD_DIM
XA_HEADS = 4
XA_W = XA_HEADS * HEAD_DIM
N_EXPERTS = 32
TOP_K = 4
D_EXPERT = D_MODEL // 4
SWIGLU_LIMIT = 7.0
SWIGLU_ALPHA = 1.702
LN_EPS = 1e-5
RMS_EPS = 1e-6
DEPTH = 1
DEEPNORM_ALPHA = (2 * DEPTH) ** 0.25
ATTN_SCALE = HEAD_DIM ** -0.5

LANES = 128
BF16_SUBLANES = 16
V7X_VMEM_BYTES = 64 * 1024 * 1024
VMEM_LIMIT = V7X_VMEM_BYTES - 8 * 1024 * 1024

LN_ROWS = 256
MM_ROWS = 2048
MM_SUB = 512
MM_COLS = 256
ATT_ROWS = 512
ATT_KEYS = 256
ATT_HEADS = 4
XA_ROWS = 512
RT_ROWS = 256
MOE_ROWS = 512
MOE_HALF = MOE_ROWS // 2
UP_COLS = 512
DOWN_COLS = 1024
DISPATCH_ROWS = 512
COMBINE_ROWS = 128
DMA_UNROLL = 8

ROW_TILES = D_MODEL // LANES


def _params(n_axes, vmem=VMEM_LIMIT):
    return pltpu.CompilerParams(dimension_semantics=("arbitrary",) * n_axes,
                                vmem_limit_bytes=vmem)


def _layer_norm_rows(xf, g, b):
    mu = jnp.mean(xf, axis=-1, keepdims=True)
    xc = xf - mu
    var = jnp.mean(xc * xc, axis=-1, keepdims=True)
    return xc * lax.rsqrt(var + LN_EPS) * g + b


def _ln_kernel(x_ref, g_ref, b_ref, *out_refs, kinds):
    y = _layer_norm_rows(x_ref[...].astype(F32), g_ref[...], b_ref[...])
    for ref, kind in zip(out_refs, kinds):
        if kind == "f32":
            ref[...] = y
        elif kind == "bf16":
            ref[...] = y.astype(BF16)
        else:
            ref[...] = y.reshape(ref.shape).astype(BF16)


def layer_norm_call(x, g, b, kinds):
    rows, d = x.shape
    out_shape, out_specs = [], []
    for kind in kinds:
        if kind == "slab":
            out_shape.append(jax.ShapeDtypeStruct((rows, ROW_TILES, LANES), BF16))
            out_specs.append(pl.BlockSpec((LN_ROWS, ROW_TILES, LANES), lambda i: (i, 0, 0)))
        else:
            out_shape.append(jax.ShapeDtypeStruct((rows, d), F32 if kind == "f32" else BF16))
            out_specs.append(pl.BlockSpec((LN_ROWS, d), lambda i: (i, 0)))
    return pl.pallas_call(
        functools.partial(_ln_kernel, kinds=kinds),
        grid=(rows // LN_ROWS,),
        in_specs=[pl.BlockSpec((LN_ROWS, d), lambda i: (i, 0)),
                  pl.BlockSpec((1, d), lambda i: (0, 0)),
                  pl.BlockSpec((1, d), lambda i: (0, 0))],
        out_specs=out_specs,
        out_shape=out_shape,
        compiler_params=_params(1),
        name="layer_norm",
    )(x, g.reshape(1, d), b.reshape(1, d))


_NT_DIMS = (((1,), (1,)), ((), ()))


def _mm_kernel(*refs, transposed, has_res, alpha, sub):
    n_pairs = len(transposed)
    lhs_refs = refs[:n_pairs]
    w_refs = refs[n_pairs:2 * n_pairs]
    pos = 2 * n_pairs
    res_ref = refs[pos] if has_res else None
    pos += int(has_res)
    out_ref = refs[pos]
    wb_refs = refs[pos + 1:]
    for w_ref, wb_ref in zip(w_refs, wb_refs):
        wb_ref[...] = w_ref[...].astype(BF16)
    rows = out_ref.shape[0]

    def body(r, carry):
        sl = pl.ds(pl.multiple_of(r * sub, sub), sub)
        acc = None
        for lhs_ref, wb_ref, tr in zip(lhs_refs, wb_refs, transposed):
            if tr:
                part = lax.dot_general(lhs_ref[sl, :], wb_ref[...], _NT_DIMS, preferred_element_type=F32)
            else:
                part = jnp.dot(lhs_ref[sl, :], wb_ref[...], preferred_element_type=F32)
            acc = part if acc is None else acc + part
        if has_res:
            acc = acc + alpha * res_ref[sl, :]
        out_ref[sl, :] = acc.astype(out_ref.dtype)
        return carry

    lax.fori_loop(0, rows // sub, body, 0)


def matmul_call(pairs, n_cols, out_dtype, res=None, alpha=1.0, tn=MM_COLS, tm=MM_ROWS, name="matmul"):
    m = pairs[0][0].shape[0]
    tm = min(tm, m)
    sub = min(MM_SUB, tm)
    in_specs, args, scratch = [], [], []
    for lhs, _, _, _, _ in pairs:
        in_specs.append(pl.BlockSpec((tm, lhs.shape[1]), lambda i, j: (i, 0)))
        args.append(lhs)
    for lhs, w, k0, col0, tr in pairs:
        k = lhs.shape[1]
        if tr:
            in_specs.append(pl.BlockSpec(
                (pl.Element(tn), pl.Element(k)),
                lambda i, j, k0=k0, col0=col0: (pl.multiple_of(col0 + j * tn, BF16_SUBLANES), k0)))
            scratch.append(pltpu.VMEM((tn, k), BF16))
        else:
            in_specs.append(pl.BlockSpec((k, tn), lambda i, j, rb=k0 // k, cb=col0 // tn: (rb, j + cb)))
            scratch.append(pltpu.VMEM((k, tn), BF16))
        args.append(w)
    if res is not None:
        in_specs.append(pl.BlockSpec((tm, tn), lambda i, j: (i, j)))
        args.append(res)
    return pl.pallas_call(
        functools.partial(_mm_kernel, transposed=tuple(p[4] for p in pairs), has_res=res is not None,
                          alpha=alpha, sub=sub),
        grid=(m // tm, n_cols // tn),
        in_specs=in_specs,
        out_specs=pl.BlockSpec((tm, tn), lambda i, j: (i, j)),
        out_shape=jax.ShapeDtypeStruct((m, n_cols), out_dtype),
        scratch_shapes=scratch,
        compiler_params=_params(2),
        name=name,
    )(*args)


def _log_sigmoid(x):
    return jnp.minimum(x, 0.0) - jnp.log1p(jnp.exp(-jnp.abs(x)))


def _split3_bf16(x):
    a = x.astype(BF16)
    r = x - a.astype(F32)
    b = r.astype(BF16)
    c = (r - b.astype(F32)).astype(BF16)
    return a, b, c


def _forget_cumsum_kernel(f_ref, b_ref, c_ref, *, chunk):
    seq = f_ref.shape[0]
    row = lax.broadcasted_iota(jnp.int32, (chunk, chunk), 0)
    col = lax.broadcasted_iota(jnp.int32, (chunk, chunk), 1)
    tri = (col <= row).astype(BF16)
    carry = jnp.zeros((1, f_ref.shape[1]), F32)
    for c0 in range(0, seq, chunk):
        lf = _log_sigmoid(f_ref[c0:c0 + chunk, :] + b_ref[...])
        part = sum(jnp.dot(tri, piece, preferred_element_type=F32) for piece in _split3_bf16(lf))
        cs = part + carry
        c_ref[c0:c0 + chunk, :] = cs
        carry = cs[chunk - 1:chunk, :]


def forget_cumsum_call(f_logit, b_f, batch, seq):
    return pl.pallas_call(
        functools.partial(_forget_cumsum_kernel, chunk=ATT_KEYS),
        grid=(batch,),
        in_specs=[pl.BlockSpec((seq, LANES), lambda b: (b, 0)),
                  pl.BlockSpec((1, LANES), lambda b: (0, 0))],
        out_specs=pl.BlockSpec((seq, LANES), lambda b: (b, 0)),
        out_shape=jax.ShapeDtypeStruct((batch * seq, LANES), F32),
        compiler_params=_params(1),
        name="forget_cumsum",
    )(f_logit, b_f)


def _rms_rows(xf, g):
    return xf * lax.rsqrt(jnp.mean(xf * xf, axis=-1, keepdims=True) + RMS_EPS) * g


def _head_cols(a):
    return slice(a * HEAD_DIM, (a + 1) * HEAD_DIM)


def _store_values_transposed(v_ref, vt_ref):
    for a in range(ATT_HEADS):
        for s0 in range(0, v_ref.shape[0], ATT_KEYS):
            vt_ref[a, :, s0:s0 + ATT_KEYS] = v_ref[s0:s0 + ATT_KEYS, _head_cols(a)].astype(F32).T.astype(BF16)


def _merge_cols(old, new, col0):
    return new if col0 == 0 else jnp.concatenate([old[:, :col0], new], axis=1)


def _fox_kernel(q_ref, k_ref, v_ref, gate_ref, ckey_ref, cqry_ref, gq_ref, gk_ref, gm_ref,
                o_ref, kn_ref, vt_ref):
    qi = pl.program_id(2)
    seq = k_ref.shape[0]
    nq = q_ref.shape[0]
    blk = ATT_KEYS
    per = nq // blk
    heads = range(ATT_HEADS)

    @pl.when(qi == 0)
    def _():
        _store_values_transposed(v_ref, vt_ref)
        for a in heads:
            for s0 in range(0, seq, blk):
                kn = _rms_rows(k_ref[s0:s0 + blk, _head_cols(a)].astype(F32), gk_ref[...])
                kn_ref[a, s0:s0 + blk, :] = kn.astype(BF16)

    qt = [(_rms_rows(q_ref[:, _head_cols(a)].astype(F32), gq_ref[...]) * ATTN_SCALE).T.astype(BF16)
          for a in heads]
    c_q = [cqry_ref[a] for a in heads]

    def step(a, j, carry, col0, masked):
        m_prev, l_prev, acc = (t[:, col0:] for t in carry)
        ks = pl.ds(pl.multiple_of(j * blk, blk), blk)
        s = jnp.dot(kn_ref[a, ks, :], qt[a][:, col0:], preferred_element_type=F32)
        s = s + (c_q[a][:, col0:] - ckey_ref[a, ks, :])
        if masked:
            key = lax.broadcasted_iota(jnp.int32, s.shape, 0)
            qry = lax.broadcasted_iota(jnp.int32, s.shape, 1)
            s = jnp.where(key <= qry, s, -jnp.inf)
        m_new = jnp.maximum(m_prev, jnp.max(s, axis=0, keepdims=True))
        p = jnp.exp(s - m_new)
        scale = jnp.exp(m_prev - m_new)
        l_new = scale * l_prev + jnp.sum(p, axis=0, keepdims=True)
        acc = scale * acc + jnp.dot(vt_ref[a, :, ks], p.astype(BF16), preferred_element_type=F32)
        return tuple(_merge_cols(old, new, col0) for old, new in zip(carry, (m_new, l_new, acc)))

    init = tuple((jnp.full((1, nq), -jnp.inf, F32), jnp.zeros((1, nq), F32),
                  jnp.zeros((HEAD_DIM, nq), F32)) for _ in heads)
    carries = lax.fori_loop(
        0, per * qi, lambda j, c: tuple(step(a, j, c[a], 0, False) for a in heads), init)
    for r in range(per):
        carries = tuple(step(a, per * qi + r, carries[a], r * blk, True) for a in heads)
    for a in heads:
        _, l_fin, acc = carries[a]
        o = _rms_rows((acc / l_fin).T, gm_ref[a]) * jax.nn.sigmoid(gate_ref[:, _head_cols(a)].astype(F32))
        o_ref[:, _head_cols(a)] = o.astype(o_ref.dtype)


def fox_attention_call(p_fox, c_col, c_row, gq, gk, gm, batch, seq):
    nq = seq // ATT_ROWS
    width = ATT_HEADS * HEAD_DIM
    groups = N_FOX_HEADS // ATT_HEADS
    return pl.pallas_call(
        _fox_kernel,
        grid=(batch, groups, nq),
        in_specs=[
            pl.BlockSpec((ATT_ROWS, width), lambda b, hp, qi: (b * nq + qi, hp)),
            pl.BlockSpec((seq, width), lambda b, hp, qi: (b, groups + hp)),
            pl.BlockSpec((seq, width), lambda b, hp, qi: (b, 2 * groups + hp)),
            pl.BlockSpec((ATT_ROWS, width), lambda b, hp, qi: (b * nq + qi, 3 * groups + hp)),
            pl.BlockSpec((None, ATT_HEADS, seq, 1), lambda b, hp, qi: (b, hp, 0, 0)),
            pl.BlockSpec((None, ATT_HEADS, 1, ATT_ROWS), lambda b, hp, qi: (b, hp, 0, qi)),
            pl.BlockSpec((1, HEAD_DIM), lambda b, hp, qi: (0, 0)),
            pl.BlockSpec((1, HEAD_DIM), lambda b, hp, qi: (0, 0)),
            pl.BlockSpec((ATT_HEADS, 1, HEAD_DIM), lambda b, hp, qi: (hp, 0, 0)),
        ],
        out_specs=pl.BlockSpec((ATT_ROWS, width), lambda b, hp, qi: (b * nq + qi, hp)),
        out_shape=jax.ShapeDtypeStruct((batch * seq, FOX_W), BF16),
        scratch_shapes=[pltpu.VMEM((ATT_HEADS, seq, HEAD_DIM), BF16), pltpu.VMEM((ATT_HEADS, HEAD_DIM, seq), BF16)],
        compiler_params=_params(3),
        name="fox_attention",
    )(p_fox, p_fox, p_fox, p_fox, c_col, c_row, gq, gk, gm)


def _sb_kernel(q_ref, k_ref, v_ref, gm_ref, o_ref, vt_ref, after_ref):
    qi = pl.program_id(2)
    nq = q_ref.shape[0]
    blk = ATT_KEYS
    per = nq // blk
    heads = range(ATT_HEADS)

    @pl.when(qi == 0)
    def _():
        _store_values_transposed(v_ref, vt_ref)
        row = lax.broadcasted_iota(jnp.int32, (blk, blk), 0)
        col = lax.broadcasted_iota(jnp.int32, (blk, blk), 1)
        after_ref[...] = (col > row).astype(BF16)

    qt = [(q_ref[:, _head_cols(a)].astype(F32) * ATTN_SCALE).T.astype(BF16) for a in heads]

    def step(a, j, carry, col0, masked):
        later, acc = (t[:, col0:] for t in carry)
        ks = pl.ds(pl.multiple_of(j * blk, blk), blk)
        z = jnp.dot(k_ref[ks, _head_cols(a)], qt[a][:, col0:], preferred_element_type=F32)
        softplus = jnp.maximum(z, 0.0) + jnp.log(1.0 + jnp.exp(-jnp.abs(z)))
        log_1m = -softplus
        if masked:
            key = lax.broadcasted_iota(jnp.int32, z.shape, 0)
            qry = lax.broadcasted_iota(jnp.int32, z.shape, 1)
            strict = key < qry
            log_1m = jnp.where(strict, log_1m, 0.0)
        hi = log_1m.astype(BF16)
        lo = (log_1m - hi.astype(F32)).astype(BF16)
        tail = (jnp.dot(after_ref[...], hi, preferred_element_type=F32)
                + jnp.dot(after_ref[...], lo, preferred_element_type=F32)) + later
        w = jnp.exp((z - softplus) + tail)
        if masked:
            w = jnp.where(strict, w, 0.0)
        acc = acc + jnp.dot(vt_ref[a, :, ks], w.astype(BF16), preferred_element_type=F32)
        later = later + jnp.sum(log_1m, axis=0, keepdims=True)
        return tuple(_merge_cols(old, new, col0) for old, new in zip(carry, (later, acc)))

    carries = tuple((jnp.zeros((1, nq), F32), jnp.zeros((HEAD_DIM, nq), F32)) for _ in heads)
    for r in reversed(range(per)):
        carries = tuple(step(a, per * qi + r, carries[a], r * blk, True) for a in heads)
    carries = lax.fori_loop(
        0, per * qi, lambda it, c: tuple(step(a, per * qi - 1 - it, c[a], 0, False) for a in heads), carries)
    for a in heads:
        o_ref[:, _head_cols(a)] = _rms_rows(carries[a][1].T, gm_ref[a]).astype(o_ref.dtype)


def sb_attention_call(p_sb, gm, batch, seq):
    nq = seq // ATT_ROWS
    width = ATT_HEADS * HEAD_DIM
    groups = N_SB_HEADS // ATT_HEADS
    return pl.pallas_call(
        _sb_kernel,
        grid=(batch, groups, nq),
        in_specs=[
            pl.BlockSpec((ATT_ROWS, width), lambda b, hp, qi: (b * nq + qi, hp)),
            pl.BlockSpec((seq, width), lambda b, hp, qi: (b, groups + hp)),
            pl.BlockSpec((seq, width), lambda b, hp, qi: (b, 2 * groups + hp)),
            pl.BlockSpec((ATT_HEADS, 1, HEAD_DIM), lambda b, hp, qi: (N_FOX_HEADS // ATT_HEADS + hp, 0, 0)),
        ],
        out_specs=pl.BlockSpec((ATT_ROWS, width), lambda b, hp, qi: (b * nq + qi, hp)),
        out_shape=jax.ShapeDtypeStruct((batch * seq, SB_W), BF16),
        scratch_shapes=[pltpu.VMEM((ATT_HEADS, HEAD_DIM, seq), BF16), pltpu.VMEM((ATT_KEYS, ATT_KEYS), BF16)],
        compiler_params=_params(3),
        name="sb_attention",
    )(p_sb, p_sb, p_sb, gm)


def _xattn_kernel(q_ref, kv_ref, o_ref):
    for hd in range(XA_HEADS):
        cols = _head_cols(hd)
        q = q_ref[:, cols]
        k = kv_ref[:, cols]
        v = kv_ref[:, XA_W + hd * HEAD_DIM:XA_W + (hd + 1) * HEAD_DIM]
        s = lax.dot_general(q, k, _NT_DIMS, preferred_element_type=F32) * ATTN_SCALE
        p = jnp.exp(s - jnp.max(s, axis=-1, keepdims=True))
        p = p / jnp.sum(p, axis=-1, keepdims=True)
        o_ref[:, cols] = jnp.dot(p.astype(BF16), v, preferred_element_type=F32).astype(o_ref.dtype)


def cross_attention_call(q, kv, seq, n_mem):
    rows = q.shape[0]
    per_batch = seq // XA_ROWS
    return pl.pallas_call(
        _xattn_kernel,
        grid=(rows // XA_ROWS,),
        in_specs=[pl.BlockSpec((XA_ROWS, XA_W), lambda i: (i, 0)),
                  pl.BlockSpec((n_mem, 2 * XA_W), lambda i: (i // per_batch, 0))],
        out_specs=pl.BlockSpec((XA_ROWS, XA_W), lambda i: (i, 0)),
        out_shape=jax.ShapeDtypeStruct((rows, XA_W), BF16),
        compiler_params=_params(1),
        name="cross_attention",
    )(q, kv)


def _router_kernel(h_ref, w_ref, b_ref, idx_ref, gate_ref, rank_ref, count_ref, run_ref):
    step = pl.program_id(0)
    rows = h_ref.shape[0]

    @pl.when(step == 0)
    def _():
        run_ref[...] = jnp.zeros_like(run_ref)

    logits = jnp.dot(h_ref[...], w_ref[...], preferred_element_type=F32,
                     precision=lax.Precision.HIGHEST) + b_ref[...]
    lane = lax.broadcasted_iota(jnp.int32, (rows, LANES), 1).astype(F32)
    work = jnp.where(lane < N_EXPERTS, logits, -jnp.inf)
    picks, values = [], []
    for _ in range(TOP_K):
        best = jnp.max(work, axis=-1, keepdims=True)
        first = jnp.min(jnp.where(work == best, lane, float(LANES)), axis=-1, keepdims=True)
        hit = lane == first
        picks.append((first, hit))
        values.append(best)
        work = jnp.where(hit, -jnp.inf, work)
    expv = [jnp.exp(v - values[0]) for v in values]
    denom = sum(expv)

    chosen = sum(hit.astype(F32) for _, hit in picks)
    r = lax.broadcasted_iota(jnp.int32, (rows, rows), 0)
    c = lax.broadcasted_iota(jnp.int32, (rows, rows), 1)
    before = (c < r).astype(BF16)
    prefix = jnp.dot(before, chosen.astype(BF16), preferred_element_type=F32) + run_ref[...]

    idx_out = jnp.zeros((rows, LANES), jnp.int32)
    gate_out = jnp.zeros((rows, LANES), F32)
    rank_out = jnp.zeros((rows, LANES), jnp.int32)
    for k, (first, hit) in enumerate(picks):
        rank_k = jnp.sum(jnp.where(hit, prefix, 0.0), axis=-1, keepdims=True).astype(jnp.int32)
        idx_out = jnp.where(lane == k, first.astype(jnp.int32), idx_out)
        gate_out = jnp.where(lane == k, expv[k] / denom, gate_out)
        rank_out = jnp.where(lane == k, rank_k, rank_out)
    idx_ref[...] = idx_out
    gate_ref[...] = gate_out
    rank_ref[...] = rank_out
    run_ref[...] += jnp.sum(chosen, axis=0, keepdims=True)
    count_ref[...] = run_ref[...]


def router_call(h, w_pad, b_pad):
    rows, d = h.shape
    tok = pl.BlockSpec((RT_ROWS, LANES), lambda i: (i, 0))
    return pl.pallas_call(
        _router_kernel,
        grid=(rows // RT_ROWS,),
        in_specs=[pl.BlockSpec((RT_ROWS, d), lambda i: (i, 0)),
                  pl.BlockSpec((d, LANES), lambda i: (0, 0)),
                  pl.BlockSpec((1, LANES), lambda i: (0, 0))],
        out_specs=[tok, tok, tok, pl.BlockSpec((1, LANES), lambda i: (0, 0))],
        out_shape=[jax.ShapeDtypeStruct((rows, LANES), jnp.int32),
                   jax.ShapeDtypeStruct((rows, LANES), F32),
                   jax.ShapeDtypeStruct((rows, LANES), jnp.int32),
                   jax.ShapeDtypeStruct((1, LANES), F32)],
        scratch_shapes=[pltpu.VMEM((1, LANES), F32)],
        compiler_params=_params(1),
        name="router",
    )(h, w_pad, b_pad)


def _slab_copy(zeros_ref, dst_hbm, sem, row):
    return pltpu.make_async_copy(zeros_ref, dst_hbm.at[pl.ds(row, MOE_ROWS)], sem)


def _token_copy(src_ref, dst_ref, sem, src_row, dst_row):
    return pltpu.make_async_copy(src_ref.at[pl.ds(src_row, 1)], dst_ref.at[pl.ds(dst_row, 1)], sem)


def _dispatch_kernel(fill_ref, dest_ref, h_ref, xs_hbm, zeros_ref, slab_sem, row_sem):
    tokens = h_ref.shape[0]

    @pl.when(pl.program_id(0) == 0)
    def _():
        zeros_ref[...] = jnp.zeros_like(zeros_ref)

        def group_slab(e, carry):
            copy = _slab_copy(zeros_ref, xs_hbm, slab_sem, fill_ref[e])
            copy.start()
            copy.wait()
            return carry

        lax.fori_loop(0, N_EXPERTS, group_slab, 0)
        tail0, n_tail = fill_ref[N_EXPERTS], fill_ref[N_EXPERTS + 1]

        def tail_start(s, carry):
            _slab_copy(zeros_ref, xs_hbm, slab_sem, tail0 + s * MOE_ROWS).start()
            return carry

        def tail_wait(s, carry):
            _slab_copy(zeros_ref, xs_hbm, slab_sem, tail0).wait()
            return carry

        lax.fori_loop(0, n_tail, tail_start, 0)
        lax.fori_loop(0, n_tail, tail_wait, 0)

    def start(t, carry):
        for k in range(TOP_K):
            _token_copy(h_ref, xs_hbm, row_sem, t, dest_ref[0, t * TOP_K + k]).start(priority=k % 2)
        return carry

    def wait(t, carry):
        for k in range(TOP_K):
            _token_copy(h_ref, xs_hbm, row_sem, t, 0).wait()
        return carry

    lax.fori_loop(0, tokens, start, 0, unroll=DMA_UNROLL)
    lax.fori_loop(0, tokens, wait, 0, unroll=DMA_UNROLL)


def dispatch_call(fill, dest, slabs, p_rows):
    tokens = slabs.shape[0]
    steps = tokens // DISPATCH_ROWS
    slab = slabs.shape[1:]
    grid_spec = pltpu.PrefetchScalarGridSpec(
        num_scalar_prefetch=1,
        grid=(steps,),
        in_specs=[pl.BlockSpec((None, 1, DISPATCH_ROWS * TOP_K), lambda i, f: (i, 0, 0), memory_space=pltpu.SMEM),
                  pl.BlockSpec((DISPATCH_ROWS,) + slab, lambda i, f: (i, 0, 0))],
        out_specs=pl.BlockSpec(memory_space=pl.ANY),
        scratch_shapes=[pltpu.VMEM((MOE_ROWS,) + slab, slabs.dtype),
                        pltpu.SemaphoreType.DMA(()), pltpu.SemaphoreType.DMA(())],
    )
    return pl.pallas_call(
        _dispatch_kernel,
        grid_spec=grid_spec,
        out_shape=jax.ShapeDtypeStruct((p_rows + MOE_ROWS,) + slab, slabs.dtype),
        compiler_params=_params(1),
        name="moe_dispatch",
    )(fill, dest.reshape(steps, 1, DISPATCH_ROWS * TOP_K), slabs)


def _tile(i, nv_ref):
    return jnp.maximum(jnp.minimum(i, nv_ref[0] - 1), 0)


def _expert_changed(i, te_ref):
    return jnp.logical_or(i == 0, te_ref[i] != te_ref[jnp.maximum(i - 1, 0)])


def _for_real_rows(i, nv_ref, tr_ref, out_ref, compute):
    used = i < nv_ref[0]
    whole = jnp.logical_and(used, tr_ref[i] > MOE_HALF)

    @pl.when(whole)
    def _():
        out_ref[...] = compute(slice(None))

    @pl.when(jnp.logical_and(used, jnp.logical_not(whole)))
    def _():
        out_ref[:MOE_HALF] = compute(slice(0, MOE_HALF))
        out_ref[MOE_HALF:] = jnp.zeros_like(out_ref[MOE_HALF:])

    @pl.when(jnp.logical_not(used))
    def _():
        out_ref[...] = jnp.zeros_like(out_ref)


def _moe_up_kernel(te_ref, nv_ref, tr_ref, x_ref, wg_ref, wu_ref, bg_ref, bu_ref, a_ref, wgb_ref, wub_ref):
    i = pl.program_id(1)

    @pl.when(_expert_changed(i, te_ref))
    def _():
        wgb_ref[...] = wg_ref[...].astype(BF16)
        wub_ref[...] = wu_ref[...].astype(BF16)

    def compute(rows):
        x = x_ref[rows, :]
        g = jnp.dot(x, wgb_ref[...], preferred_element_type=F32) + bg_ref[...]
        u = jnp.dot(x, wub_ref[...], preferred_element_type=F32) + bu_ref[...]
        g = jnp.minimum(g, SWIGLU_LIMIT)
        u = jnp.clip(u, -SWIGLU_LIMIT, SWIGLU_LIMIT)
        return ((u + 1.0) * g * jax.nn.sigmoid(SWIGLU_ALPHA * g)).astype(a_ref.dtype)

    _for_real_rows(i, nv_ref, tr_ref, a_ref, compute)


def moe_up_call(tile_expert, n_valid, tile_rows, xs, w_up, b_up):
    d = xs.shape[1]
    n_tiles = tile_expert.shape[0]
    p_rows = n_tiles * MOE_ROWS
    n_col = D_EXPERT // UP_COLS
    grid_spec = pltpu.PrefetchScalarGridSpec(
        num_scalar_prefetch=3,
        grid=(n_col, n_tiles),
        in_specs=[
            pl.BlockSpec((MOE_ROWS, d), lambda n, i, te, nv, tr: (_tile(i, nv), 0)),
            pl.BlockSpec((None, d, UP_COLS), lambda n, i, te, nv, tr: (te[_tile(i, nv)], 0, n)),
            pl.BlockSpec((None, d, UP_COLS), lambda n, i, te, nv, tr: (te[_tile(i, nv)], 0, n_col + n)),
            pl.BlockSpec((None, 1, UP_COLS), lambda n, i, te, nv, tr: (te[_tile(i, nv)], 0, n)),
            pl.BlockSpec((None, 1, UP_COLS), lambda n, i, te, nv, tr: (te[_tile(i, nv)], 0, n_col + n)),
        ],
        out_specs=pl.BlockSpec((MOE_ROWS, UP_COLS), lambda n, i, te, nv, tr: (i, n)),
        scratch_shapes=[pltpu.VMEM((d, UP_COLS), BF16), pltpu.VMEM((d, UP_COLS), BF16)],
    )
    return pl.pallas_call(
        _moe_up_kernel,
        grid_spec=grid_spec,
        out_shape=jax.ShapeDtypeStruct((p_rows, D_EXPERT), BF16),
        compiler_params=_params(2),
        name="moe_up",
    )(tile_expert, n_valid, tile_rows, xs, w_up, w_up, b_up, b_up)


def _moe_down_kernel(te_ref, nv_ref, tr_ref, a_ref, w_ref, b_ref, y_ref, wb_ref):
    i = pl.program_id(1)

    @pl.when(_expert_changed(i, te_ref))
    def _():
        wb_ref[...] = w_ref[...].astype(BF16)

    def compute(rows):
        y = jnp.dot(a_ref[rows, :], wb_ref[...], preferred_element_type=F32) + b_ref[...]
        return y.reshape((y.shape[0],) + y_ref.shape[1:])

    _for_real_rows(i, nv_ref, tr_ref, y_ref, compute)


def moe_down_call(tile_expert, n_valid, tile_rows, act, w_down, b_down):
    p_rows = act.shape[0]
    n_tiles = p_rows // MOE_ROWS
    d_out = w_down.shape[2]
    grid_spec = pltpu.PrefetchScalarGridSpec(
        num_scalar_prefetch=3,
        grid=(d_out // DOWN_COLS, n_tiles),
        in_specs=[
            pl.BlockSpec((MOE_ROWS, D_EXPERT), lambda n, i, te, nv, tr: (_tile(i, nv), 0)),
            pl.BlockSpec((None, D_EXPERT, DOWN_COLS), lambda n, i, te, nv, tr: (te[_tile(i, nv)], 0, n)),
            pl.BlockSpec((None, 1, DOWN_COLS), lambda n, i, te, nv, tr: (te[_tile(i, nv)], 0, n)),
        ],
        out_specs=pl.BlockSpec((MOE_ROWS, DOWN_COLS // LANES, LANES), lambda n, i, te, nv, tr: (i, n, 0)),
        scratch_shapes=[pltpu.VMEM((D_EXPERT, DOWN_COLS), BF16)],
    )
    return pl.pallas_call(
        _moe_down_kernel,
        grid_spec=grid_spec,
        out_shape=jax.ShapeDtypeStruct((p_rows, d_out // LANES, LANES), F32),
        compiler_params=_params(2),
        name="moe_down",
    )(tile_expert, n_valid, tile_rows, act, w_down, b_down)


def _combine_kernel(dest_ref, next_ref, y_hbm, gate_ref, h_ref, g_ref, b_ref, o_ref, buf_ref, sems):
    step = pl.program_id(0)
    rows = h_ref.shape[0]
    slot = step % 2

    def issue(idx_ref, to_slot):
        def start(r, carry):
            for k in range(TOP_K):
                _token_copy(y_hbm, buf_ref.at[to_slot, k], sems.at[to_slot], idx_ref[0, r * TOP_K + k], r
                            ).start(priority=k % 2)
            return carry
        lax.fori_loop(0, rows, start, 0, unroll=DMA_UNROLL)

    @pl.when(step == 0)
    def _():
        issue(dest_ref, 0)

    @pl.when(step + 1 < pl.num_programs(0))
    def _():
        issue(next_ref, 1 - slot)

    def wait(r, carry):
        for k in range(TOP_K):
            _token_copy(y_hbm, buf_ref.at[slot, k], sems.at[slot], 0, r).wait()
        return carry

    lax.fori_loop(0, rows, wait, 0, unroll=DMA_UNROLL)
    ff = None
    for k in range(TOP_K):
        term = gate_ref[:, k:k + 1] * buf_ref[slot, k].reshape(rows, D_MODEL)
        ff = term if ff is None else ff + term
    o_ref[...] = _layer_norm_rows(DEEPNORM_ALPHA * h_ref[...] + ff, g_ref[...], b_ref[...])


def combine_call(dest, y, gate, h, g, b):
    rows, d = h.shape
    steps = rows // COMBINE_ROWS
    dest3 = dest.reshape(steps, 1, COMBINE_ROWS * TOP_K)
    idx_block = (None, 1, COMBINE_ROWS * TOP_K)
    return pl.pallas_call(
        _combine_kernel,
        grid=(steps,),
        in_specs=[pl.BlockSpec(idx_block, lambda i: (i, 0, 0), memory_space=pltpu.SMEM),
                  pl.BlockSpec(idx_block, lambda i: (jnp.minimum(i + 1, steps - 1), 0, 0), memory_space=pltpu.SMEM),
                  pl.BlockSpec(memory_space=pl.ANY),
                  pl.BlockSpec((COMBINE_ROWS, LANES), lambda i: (i, 0)),
                  pl.BlockSpec((COMBINE_ROWS, d), lambda i: (i, 0)),
                  pl.BlockSpec((1, d), lambda i: (0, 0)),
                  pl.BlockSpec((1, d), lambda i: (0, 0))],
        out_specs=pl.BlockSpec((COMBINE_ROWS, d), lambda i: (i, 0)),
        out_shape=jax.ShapeDtypeStruct((rows, d), F32),
        scratch_shapes=[pltpu.VMEM((2, TOP_K, COMBINE_ROWS, ROW_TILES, LANES), F32),
                        pltpu.SemaphoreType.DMA((2,))],
        compiler_params=_params(1),
        name="moe_combine",
    )(dest3, dest3, y, gate, h, g.reshape(1, d), b.reshape(1, d))


def _pad_lanes(a, rows):
    out = jnp.zeros((rows, LANES), a.dtype)
    return out.at[:, :a.shape[-1]].set(a.reshape(rows, -1))


def kernel(x, mem, ln_in_g, ln_in_b, w_in, b_f, fox_q_norm_g, fox_k_norm_g, mix_norm_g, w_out, ln_mix_g, ln_mix_b, mem_ln_g, mem_ln_b, xa_wq, xa_wkv, xa_wo, ln_xa_g, ln_xa_b, router_w, router_b, w_up, b_up, w_down, b_down, ln_moe_g, ln_moe_b):
    batch, seq, d = x.shape
    n_mem = mem.shape[1]
    tokens = batch * seq
    assert w_in.shape[0] == DEPTH == 1 and d == D_MODEL

    h0, h0b = layer_norm_call(x.reshape(tokens, d), ln_in_g, ln_in_b, ("f32", "bf16"))
    w_in_t = jnp.swapaxes(w_in, 1, 2)[0]
    sb0 = 4 * FOX_W + N_FOX_HEADS
    p_fox = matmul_call([(h0b, w_in_t, 0, 0, True)], 4 * FOX_W, BF16, name="in_proj_fox")
    f_logit = matmul_call([(h0b, w_in_t, 0, 4 * FOX_W, True)], LANES, F32, tn=LANES, name="in_proj_forget")
    p_sb = matmul_call([(h0b, w_in_t, 0, sb0, True)], 3 * SB_W, BF16, name="in_proj_sb")

    c = forget_cumsum_call(f_logit, _pad_lanes(b_f[0], 1), batch, seq)
    c = c.reshape(batch, seq, LANES)[:, :, :N_FOX_HEADS].transpose(0, 2, 1)
    gm = mix_norm_g[0].reshape(-1, 1, HEAD_DIM)
    o_f = fox_attention_call(p_fox, c[..., None], c[:, :, None, :],
                             fox_q_norm_g[0].reshape(1, HEAD_DIM), fox_k_norm_g[0].reshape(1, HEAD_DIM),
                             gm, batch, seq)
    o_s = sb_attention_call(p_sb, gm, batch, seq)
    r1 = matmul_call([(o_f, w_out[0], 0, 0, False), (o_s, w_out[0], FOX_W, 0, False)], d, F32,
                     res=h0, alpha=DEEPNORM_ALPHA, name="out_proj")
    h1, h1b = layer_norm_call(r1, ln_mix_g[0], ln_mix_b[0], ("f32", "bf16"))

    (mem_nb,) = layer_norm_call(mem.reshape(batch * n_mem, d), mem_ln_g[0], mem_ln_b[0], ("bf16",))
    kv = matmul_call([(mem_nb, xa_wkv[0], 0, 0, False)], 2 * XA_W, BF16, name="xa_kv_proj")
    qx = matmul_call([(h1b, xa_wq[0], 0, 0, False)], XA_W, BF16, name="xa_q_proj")
    ox = cross_attention_call(qx, kv, seq, n_mem)
    r2 = matmul_call([(ox, xa_wo[0], 0, 0, False)], d, F32, res=h1, alpha=DEEPNORM_ALPHA, name="xa_out_proj")
    h2, h2s = layer_norm_call(r2, ln_xa_g[0], ln_xa_b[0], ("f32", "slab"))

    idx, gate, rank, counts = router_call(h2, _pad_lanes(router_w[0], d), _pad_lanes(router_b[0], 1))
    counts = counts[0, :N_EXPERTS].astype(jnp.int32)
    padded = (counts + MOE_ROWS - 1) // MOE_ROWS * MOE_ROWS
    ends = jnp.cumsum(padded)
    starts = ends - padded
    dest = (starts[idx[:, :TOP_K]] + rank[:, :TOP_K]).reshape(-1)
    p_rows = tokens * TOP_K + N_EXPERTS * MOE_ROWS
    n_tiles = p_rows // MOE_ROWS
    tile_row0 = jnp.arange(n_tiles, dtype=jnp.int32) * MOE_ROWS
    tile_expert = jnp.minimum(jnp.sum(ends[None, :] <= tile_row0[:, None], axis=1), N_EXPERTS - 1).astype(jnp.int32)
    tile_rows = jnp.clip(counts[tile_expert] - (tile_row0 - starts[tile_expert]), 0, MOE_ROWS).astype(jnp.int32)
    n_valid = (ends[-1:] // MOE_ROWS).astype(jnp.int32)
    fill = jnp.concatenate([starts + counts, ends[-1:], (p_rows + MOE_ROWS - ends[-1:]) // MOE_ROWS]).astype(jnp.int32)

    xs = dispatch_call(fill, dest, h2s, p_rows).reshape(p_rows + MOE_ROWS, d)
    act = moe_up_call(tile_expert, n_valid, tile_rows, xs, w_up[0], b_up[0].reshape(N_EXPERTS, 1, -1))
    y = moe_down_call(tile_expert, n_valid, tile_rows, act, w_down[0], b_down[0].reshape(N_EXPERTS, 1, -1))
    out = combine_call(dest, y, gate, h2, ln_moe_g[0], ln_moe_b[0])
    return out.reshape(batch, seq, d)
```

```python
import functools

import jax
import jax.numpy as jnp
from jax import lax
from jax.experimental import pallas as pl
from jax.experimental.pallas import tpu as pltpu

F32 = jnp.float32
BF16 = jnp.bfloat16

D_MODEL = 4096
HEAD_DIM = 128
N_FOX_HEADS = 16
N_SB_HEADS = 16
FOX_W = N_FOX_HEADS * HEAD_DIM
SB_W = N_SB_HEADS * HEAD_DIM
XA_HEADS = 4
XA_W = XA_HEADS * HEAD_DIM
N_EXPERTS = 32
TOP_K = 4
D_EXPERT = D_MODEL // 4
SWIGLU_LIMIT = 7.0
SWIGLU_ALPHA = 1.702
LN_EPS = 1e-5
RMS_EPS = 1e-6
DEPTH = 1
DEEPNORM_ALPHA = (2 * DEPTH) ** 0.25
ATTN_SCALE = HEAD_DIM ** -0.5

LANES = 128
BF16_SUBLANES = 16
V7X_VMEM_BYTES = 64 * 1024 * 1024
VMEM_LIMIT = V7X_VMEM_BYTES - 8 * 1024 * 1024

LN_ROWS = 256
MM_ROWS = 2048
MM_SUB = 1024
MM_COLS = 256
ATT_ROWS = 512
ATT_KEYS = 256
ATT_HEADS = 4
XA_ROWS = 512
RT_ROWS = 256
MOE_ROWS = 512
MOE_HALF = MOE_ROWS // 2
UP_COLS = 512
DOWN_COLS = 1024
DISPATCH_ROWS = 512
COMBINE_ROWS = 128
DMA_UNROLL = 8

ROW_TILES = D_MODEL // LANES


def _params(n_axes, vmem=VMEM_LIMIT):
    return pltpu.CompilerParams(dimension_semantics=("arbitrary",) * n_axes,
                                vmem_limit_bytes=vmem)


def _layer_norm_rows(xf, g, b):
    mu = jnp.mean(xf, axis=-1, keepdims=True)
    xc = xf - mu
    var = jnp.mean(xc * xc, axis=-1, keepdims=True)
    return xc * lax.rsqrt(var + LN_EPS) * g + b


def _ln_kernel(x_ref, g_ref, b_ref, *out_refs, kinds):
    y = _layer_norm_rows(x_ref[...].astype(F32), g_ref[...], b_ref[...])
    for ref, kind in zip(out_refs, kinds):
        if kind == "f32":
            ref[...] = y
        elif kind == "bf16":
            ref[...] = y.astype(BF16)
        else:
            ref[...] = y.reshape(ref.shape).astype(BF16)


def layer_norm_call(x, g, b, kinds):
    rows, d = x.shape
    out_shape, out_specs = [], []
    for kind in kinds:
        if kind == "slab":
            out_shape.append(jax.ShapeDtypeStruct((rows, ROW_TILES, LANES), BF16))
            out_specs.append(pl.BlockSpec((LN_ROWS, ROW_TILES, LANES), lambda i: (i, 0, 0)))
        else:
            out_shape.append(jax.ShapeDtypeStruct((rows, d), F32 if kind == "f32" else BF16))
            out_specs.append(pl.BlockSpec((LN_ROWS, d), lambda i: (i, 0)))
    return pl.pallas_call(
        functools.partial(_ln_kernel, kinds=kinds),
        grid=(rows // LN_ROWS,),
        in_specs=[pl.BlockSpec((LN_ROWS, d), lambda i: (i, 0)),
                  pl.BlockSpec((1, d), lambda i: (0, 0)),
                  pl.BlockSpec((1, d), lambda i: (0, 0))],
        out_specs=out_specs,
        out_shape=out_shape,
        compiler_params=_params(1),
        name="layer_norm",
    )(x, g.reshape(1, d), b.reshape(1, d))


_NT_DIMS = (((1,), (1,)), ((), ()))


def _mm_kernel(*refs, transposed, has_res, alpha, sub):
    n_pairs = len(transposed)
    lhs_refs = refs[:n_pairs]
    w_refs = refs[n_pairs:2 * n_pairs]
    pos = 2 * n_pairs
    res_ref = refs[pos] if has_res else None
    pos += int(has_res)
    out_ref = refs[pos]
    wb_refs = refs[pos + 1:]
    for w_ref, wb_ref in zip(w_refs, wb_refs):
        wb_ref[...] = w_ref[...].astype(BF16)
    rows = out_ref.shape[0]

    def body(r, carry):
        sl = pl.ds(pl.multiple_of(r * sub, sub), sub)
        acc = None
        for lhs_ref, wb_ref, tr in zip(lhs_refs, wb_refs, transposed):
            if tr:
                part = lax.dot_general(lhs_ref[sl, :], wb_ref[...], _NT_DIMS, preferred_element_type=F32)
            else:
                part = jnp.dot(lhs_ref[sl, :], wb_ref[...], preferred_element_type=F32)
            acc = part if acc is None else acc + part
        if has_res:
            acc = acc + alpha * res_ref[sl, :]
        out_ref[sl, :] = acc.astype(out_ref.dtype)
        return carry

    lax.fori_loop(0, rows // sub, body, 0)


def matmul_call(pairs, n_cols, out_dtype, res=None, alpha=1.0, tn=MM_COLS, tm=MM_ROWS, name="matmul"):
    m = pairs[0][0].shape[0]
    tm = min(tm, m)
    sub = min(MM_SUB, tm)
    in_specs, args, scratch = [], [], []
    for lhs, _, _, _, _ in pairs:
        in_specs.append(pl.BlockSpec((tm, lhs.shape[1]), lambda i, j: (i, 0)))
        args.append(lhs)
    for lhs, w, k0, col0, tr in pairs:
        k = lhs.shape[1]
        if tr:
            in_specs.append(pl.BlockSpec(
                (pl.Element(tn), pl.Element(k)),
                lambda i, j, k0=k0, col0=col0: (pl.multiple_of(col0 + j * tn, BF16_SUBLANES), k0)))
            scratch.append(pltpu.VMEM((tn, k), BF16))
        else:
            in_specs.append(pl.BlockSpec((k, tn), lambda i, j, rb=k0 // k, cb=col0 // tn: (rb, j + cb)))
            scratch.append(pltpu.VMEM((k, tn), BF16))
        args.append(w)
    if res is not None:
        in_specs.append(pl.BlockSpec((tm, tn), lambda i, j: (i, j)))
        args.append(res)
    return pl.pallas_call(
        functools.partial(_mm_kernel, transposed=tuple(p[4] for p in pairs), has_res=res is not None,
                          alpha=alpha, sub=sub),
        grid=(m // tm, n_cols // tn),
        in_specs=in_specs,
        out_specs=pl.BlockSpec((tm, tn), lambda i, j: (i, j)),
        out_shape=jax.ShapeDtypeStruct((m, n_cols), out_dtype),
        scratch_shapes=scratch,
        compiler_params=_params(2),
        name=name,
    )(*args)


def _log_sigmoid(x):
    return jnp.minimum(x, 0.0) - jnp.log1p(jnp.exp(-jnp.abs(x)))


def _split3_bf16(x):
    a = x.astype(BF16)
    r = x - a.astype(F32)
    b = r.astype(BF16)
    c = (r - b.astype(F32)).astype(BF16)
    return a, b, c


def _forget_cumsum_kernel(f_ref, b_ref, c_ref, *, chunk):
    seq = f_ref.shape[0]
    row = lax.broadcasted_iota(jnp.int32, (chunk, chunk), 0)
    col = lax.broadcasted_iota(jnp.int32, (chunk, chunk), 1)
    tri = (col <= row).astype(BF16)
    carry = jnp.zeros((1, f_ref.shape[1]), F32)
    for c0 in range(0, seq, chunk):
        lf = _log_sigmoid(f_ref[c0:c0 + chunk, :] + b_ref[...])
        part = sum(jnp.dot(tri, piece, preferred_element_type=F32) for piece in _split3_bf16(lf))
        cs = part + carry
        c_ref[c0:c0 + chunk, :] = cs
        carry = cs[chunk - 1:chunk, :]


def forget_cumsum_call(f_logit, b_f, batch, seq):
    return pl.pallas_call(
        functools.partial(_forget_cumsum_kernel, chunk=ATT_KEYS),
        grid=(batch,),
        in_specs=[pl.BlockSpec((seq, LANES), lambda b: (b, 0)),
                  pl.BlockSpec((1, LANES), lambda b: (0, 0))],
        out_specs=pl.BlockSpec((seq, LANES), lambda b: (b, 0)),
        out_shape=jax.ShapeDtypeStruct((batch * seq, LANES), F32),
        compiler_params=_params(1),
        name="forget_cumsum",
    )(f_logit, b_f)


def _rms_rows(xf, g):
    return xf * lax.rsqrt(jnp.mean(xf * xf, axis=-1, keepdims=True) + RMS_EPS) * g


def _head_cols(a):
    return slice(a * HEAD_DIM, (a + 1) * HEAD_DIM)


def _store_values_transposed(v_ref, vt_ref):
    for a in range(ATT_HEADS):
        for s0 in range(0, v_ref.shape[0], ATT_KEYS):
            vt_ref[a, :, s0:s0 + ATT_KEYS] = v_ref[s0:s0 + ATT_KEYS, _head_cols(a)].astype(F32).T.astype(BF16)


def _fox_kernel(q_ref, k_ref, v_ref, gate_ref, ckey_ref, cqry_ref, gq_ref, gk_ref, gm_ref,
                o_ref, kn_ref, vt_ref, m_ref, l_ref, acc_ref):
    qi = pl.program_id(2)
    seq = k_ref.shape[0]
    nq = q_ref.shape[0]
    blk = ATT_KEYS
    per = nq // blk
    heads = range(ATT_HEADS)

    @pl.when(qi == 0)
    def _():
        _store_values_transposed(v_ref, vt_ref)
        for a in heads:
            for s0 in range(0, seq, blk):
                kn = _rms_rows(k_ref[s0:s0 + blk, _head_cols(a)].astype(F32), gk_ref[...])
                kn_ref[a, s0:s0 + blk, :] = kn.astype(BF16)

    qt = [(_rms_rows(q_ref[:, _head_cols(a)].astype(F32), gq_ref[...]) * ATTN_SCALE).T.astype(BF16)
          for a in heads]
    c_q = [cqry_ref[a] for a in heads]

    def step(a, j, col0, masked):
        m_prev = m_ref[a, :, col0:]
        ks = pl.ds(pl.multiple_of(j * blk, blk), blk)
        s = jnp.dot(kn_ref[a, ks, :], qt[a][:, col0:], preferred_element_type=F32)
        s = s + (c_q[a][:, col0:] - ckey_ref[a, ks, :])
        if masked:
            key = lax.broadcasted_iota(jnp.int32, s.shape, 0)
            qry = lax.broadcasted_iota(jnp.int32, s.shape, 1)
            s = jnp.where(key <= qry, s, -jnp.inf)
        m_new = jnp.maximum(m_prev, jnp.max(s, axis=0, keepdims=True))
        p = jnp.exp(s - m_new)
        scale = jnp.exp(m_prev - m_new)
        m_ref[a, :, col0:] = m_new
        l_ref[a, :, col0:] = scale * l_ref[a, :, col0:] + jnp.sum(p, axis=0, keepdims=True)
        acc_ref[a, :, col0:] = scale * acc_ref[a, :, col0:] + jnp.dot(
            vt_ref[a, :, ks], p.astype(BF16), preferred_element_type=F32)

    m_ref[...] = jnp.full(m_ref.shape, -jnp.inf, F32)
    l_ref[...] = jnp.zeros_like(l_ref)
    acc_ref[...] = jnp.zeros_like(acc_ref)

    def full_blocks(it, carry):
        for r in range(per):
            for a in heads:
                step(a, per * it + r, 0, False)
        return carry

    lax.fori_loop(0, qi, full_blocks, 0)
    for r in range(per):
        for a in heads:
            step(a, per * qi + r, r * blk, True)
    for a in heads:
        o = _rms_rows((acc_ref[a] / l_ref[a]).T, gm_ref[a]) * jax.nn.sigmoid(gate_ref[:, _head_cols(a)].astype(F32))
        o_ref[:, _head_cols(a)] = o.astype(o_ref.dtype)


def fox_attention_call(p_fox, c_col, c_row, gq, gk, gm, batch, seq):
    nq = seq // ATT_ROWS
    width = ATT_HEADS * HEAD_DIM
    groups = N_FOX_HEADS // ATT_HEADS
    return pl.pallas_call(
        _fox_kernel,
        grid=(batch, groups, nq),
        in_specs=[
            pl.BlockSpec((ATT_ROWS, width), lambda b, hp, qi: (b * nq + qi, hp)),
            pl.BlockSpec((seq, width), lambda b, hp, qi: (b, groups + hp)),
            pl.BlockSpec((seq, width), lambda b, hp, qi: (b, 2 * groups + hp)),
            pl.BlockSpec((ATT_ROWS, width), lambda b, hp, qi: (b * nq + qi, 3 * groups + hp)),
            pl.BlockSpec((None, ATT_HEADS, seq, 1), lambda b, hp, qi: (b, hp, 0, 0)),
            pl.BlockSpec((None, ATT_HEADS, 1, ATT_ROWS), lambda b, hp, qi: (b, hp, 0, qi)),
            pl.BlockSpec((1, HEAD_DIM), lambda b, hp, qi: (0, 0)),
            pl.BlockSpec((1, HEAD_DIM), lambda b, hp, qi: (0, 0)),
            pl.BlockSpec((ATT_HEADS, 1, HEAD_DIM), lambda b, hp, qi: (hp, 0, 0)),
        ],
        out_specs=pl.BlockSpec((ATT_ROWS, width), lambda b, hp, qi: (b * nq + qi, hp)),
        out_shape=jax.ShapeDtypeStruct((batch * seq, FOX_W), BF16),
        scratch_shapes=[pltpu.VMEM((ATT_HEADS, seq, HEAD_DIM), BF16), pltpu.VMEM((ATT_HEADS, HEAD_DIM, seq), BF16),
                        pltpu.VMEM((ATT_HEADS, 1, ATT_ROWS), F32), pltpu.VMEM((ATT_HEADS, 1, ATT_ROWS), F32),
                        pltpu.VMEM((ATT_HEADS, HEAD_DIM, ATT_ROWS), F32)],
        compiler_params=_params(3),
        name="fox_attention",
    )(p_fox, p_fox, p_fox, p_fox, c_col, c_row, gq, gk, gm)


def _sb_kernel(q_ref, k_ref, v_ref, gm_ref, o_ref, vt_ref, after_ref, later_ref, acc_ref):
    qi = pl.program_id(2)
    nq = q_ref.shape[0]
    blk = ATT_KEYS
    per = nq // blk
    heads = range(ATT_HEADS)

    @pl.when(qi == 0)
    def _():
        _store_values_transposed(v_ref, vt_ref)
        row = lax.broadcasted_iota(jnp.int32, (blk, blk), 0)
        col = lax.broadcasted_iota(jnp.int32, (blk, blk), 1)
        after_ref[...] = (col > row).astype(BF16)

    qt = [(q_ref[:, _head_cols(a)].astype(F32) * ATTN_SCALE).T.astype(BF16) for a in heads]

    def step(a, j, col0, masked):
        ks = pl.ds(pl.multiple_of(j * blk, blk), blk)
        z = jnp.dot(k_ref[ks, _head_cols(a)], qt[a][:, col0:], preferred_element_type=F32)
        softplus = jnp.maximum(z, 0.0) + jnp.log(1.0 + jnp.exp(-jnp.abs(z)))
        log_1m = -softplus
        if masked:
            key = lax.broadcasted_iota(jnp.int32, z.shape, 0)
            qry = lax.broadcasted_iota(jnp.int32, z.shape, 1)
            strict = key < qry
            log_1m = jnp.where(strict, log_1m, 0.0)
        hi = log_1m.astype(BF16)
        lo = (log_1m - hi.astype(F32)).astype(BF16)
        later = later_ref[a, :, col0:]
        tail = (jnp.dot(after_ref[...], hi, preferred_element_type=F32)
                + jnp.dot(after_ref[...], lo, preferred_element_type=F32)) + later
        w = jnp.exp((z - softplus) + tail)
        if masked:
            w = jnp.where(strict, w, 0.0)
        acc_ref[a, :, col0:] += jnp.dot(vt_ref[a, :, ks], w.astype(BF16), preferred_element_type=F32)
        later_ref[a, :, col0:] = later + jnp.sum(log_1m, axis=0, keepdims=True)

    later_ref[...] = jnp.zeros_like(later_ref)
    acc_ref[...] = jnp.zeros_like(acc_ref)
    for r in reversed(range(per)):
        for a in heads:
            step(a, per * qi + r, r * blk, True)

    def full_blocks(it, carry):
        for r in range(per):
            for a in heads:
                step(a, per * (qi - it) - 1 - r, 0, False)
        return carry

    lax.fori_loop(0, qi, full_blocks, 0)
    for a in heads:
        o_ref[:, _head_cols(a)] = _rms_rows(acc_ref[a].T, gm_ref[a]).astype(o_ref.dtype)


def sb_attention_call(p_sb, gm, batch, seq):
    nq = seq // ATT_ROWS
    width = ATT_HEADS * HEAD_DIM
    groups = N_SB_HEADS // ATT_HEADS
    return pl.pallas_call(
        _sb_kernel,
        grid=(batch, groups, nq),
        in_specs=[
            pl.BlockSpec((ATT_ROWS, width), lambda b, hp, qi: (b * nq + qi, hp)),
            pl.BlockSpec((seq, width), lambda b, hp, qi: (b, groups + hp)),
            pl.BlockSpec((seq, width), lambda b, hp, qi: (b, 2 * groups + hp)),
            pl.BlockSpec((ATT_HEADS, 1, HEAD_DIM), lambda b, hp, qi: (N_FOX_HEADS // ATT_HEADS + hp, 0, 0)),
        ],
        out_specs=pl.BlockSpec((ATT_ROWS, width), lambda b, hp, qi: (b * nq + qi, hp)),
        out_shape=jax.ShapeDtypeStruct((batch * seq, SB_W), BF16),
        scratch_shapes=[pltpu.VMEM((ATT_HEADS, HEAD_DIM, seq), BF16), pltpu.VMEM((ATT_KEYS, ATT_KEYS), BF16),
                        pltpu.VMEM((ATT_HEADS, 1, ATT_ROWS), F32), pltpu.VMEM((ATT_HEADS, HEAD_DIM, ATT_ROWS), F32)],
        compiler_params=_params(3),
        name="sb_attention",
    )(p_sb, p_sb, p_sb, gm)


def _xattn_kernel(q_ref, kv_ref, o_ref):
    for hd in range(XA_HEADS):
        cols = _head_cols(hd)
        q = q_ref[:, cols]
        k = kv_ref[:, cols]
        v = kv_ref[:, XA_W + hd * HEAD_DIM:XA_W + (hd + 1) * HEAD_DIM]
        s = lax.dot_general(q, k, _NT_DIMS, preferred_element_type=F32) * ATTN_SCALE
        p = jnp.exp(s - jnp.max(s, axis=-1, keepdims=True))
        p = p / jnp.sum(p, axis=-1, keepdims=True)
        o_ref[:, cols] = jnp.dot(p.astype(BF16), v, preferred_element_type=F32).astype(o_ref.dtype)


def cross_attention_call(q, kv, seq, n_mem):
    rows = q.shape[0]
    per_batch = seq // XA_ROWS
    return pl.pallas_call(
        _xattn_kernel,
        grid=(rows // XA_ROWS,),
        in_specs=[pl.BlockSpec((XA_ROWS, XA_W), lambda i: (i, 0)),
                  pl.BlockSpec((n_mem, 2 * XA_W), lambda i: (i // per_batch, 0))],
        out_specs=pl.BlockSpec((XA_ROWS, XA_W), lambda i: (i, 0)),
        out_shape=jax.ShapeDtypeStruct((rows, XA_W), BF16),
        compiler_params=_params(1),
        name="cross_attention",
    )(q, kv)


def _router_kernel(h_ref, w_ref, b_ref, idx_ref, gate_ref, rank_ref, count_ref, run_ref):
    step = pl.program_id(0)
    rows = h_ref.shape[0]

    @pl.when(step == 0)
    def _():
        run_ref[...] = jnp.zeros_like(run_ref)

    logits = jnp.dot(h_ref[...], w_ref[...], preferred_element_type=F32,
                     precision=lax.Precision.HIGHEST) + b_ref[...]
    lane = lax.broadcasted_iota(jnp.int32, (rows, LANES), 1).astype(F32)
    work = jnp.where(lane < N_EXPERTS, logits, -jnp.inf)
    picks, values = [], []
    for _ in range(TOP_K):
        best = jnp.max(work, axis=-1, keepdims=True)
        first = jnp.min(jnp.where(work == best, lane, float(LANES)), axis=-1, keepdims=True)
        hit = lane == first
        picks.append((first, hit))
        values.append(best)
        work = jnp.where(hit, -jnp.inf, work)
    expv = [jnp.exp(v - values[0]) for v in values]
    denom = sum(expv)

    chosen = sum(hit.astype(F32) for _, hit in picks)
    r = lax.broadcasted_iota(jnp.int32, (rows, rows), 0)
    c = lax.broadcasted_iota(jnp.int32, (rows, rows), 1)
    before = (c < r).astype(BF16)
    prefix = jnp.dot(before, chosen.astype(BF16), preferred_element_type=F32) + run_ref[...]

    idx_out = jnp.zeros((rows, LANES), jnp.int32)
    gate_out = jnp.zeros((rows, LANES), F32)
    rank_out = jnp.zeros((rows, LANES), jnp.int32)
    for k, (first, hit) in enumerate(picks):
        rank_k = jnp.sum(jnp.where(hit, prefix, 0.0), axis=-1, keepdims=True).astype(jnp.int32)
        idx_out = jnp.where(lane == k, first.astype(jnp.int32), idx_out)
        gate_out = jnp.where(lane == k, expv[k] / denom, gate_out)
        rank_out = jnp.where(lane == k, rank_k, rank_out)
    idx_ref[...] = idx_out
    gate_ref[...] = gate_out
    rank_ref[...] = rank_out
    run_ref[...] += jnp.sum(chosen, axis=0, keepdims=True)
    count_ref[...] = run_ref[...]


def router_call(h, w_pad, b_pad):
    rows, d = h.shape
    tok = pl.BlockSpec((RT_ROWS, LANES), lambda i: (i, 0))
    return pl.pallas_call(
        _router_kernel,
        grid=(rows // RT_ROWS,),
        in_specs=[pl.BlockSpec((RT_ROWS, d), lambda i: (i, 0)),
                  pl.BlockSpec((d, LANES), lambda i: (0, 0)),
                  pl.BlockSpec((1, LANES), lambda i: (0, 0))],
        out_specs=[tok, tok, tok, pl.BlockSpec((1, LANES), lambda i: (0, 0))],
        out_shape=[jax.ShapeDtypeStruct((rows, LANES), jnp.int32),
                   jax.ShapeDtypeStruct((rows, LANES), F32),
                   jax.ShapeDtypeStruct((rows, LANES), jnp.int32),
                   jax.ShapeDtypeStruct((1, LANES), F32)],
        scratch_shapes=[pltpu.VMEM((1, LANES), F32)],
        compiler_params=_params(1),
        name="router",
    )(h, w_pad, b_pad)


def _slab_copy(zeros_ref, dst_hbm, sem, row):
    return pltpu.make_async_copy(zeros_ref, dst_hbm.at[pl.ds(row, MOE_ROWS)], sem)


def _token_copy(src_ref, dst_ref, sem, src_row, dst_row):
    return pltpu.make_async_copy(src_ref.at[pl.ds(src_row, 1)], dst_ref.at[pl.ds(dst_row, 1)], sem)


def _dispatch_kernel(fill_ref, dest_ref, h_ref, xs_hbm, zeros_ref, slab_sem, row_sem):
    tokens = h_ref.shape[0]

    @pl.when(pl.program_id(0) == 0)
    def _():
        zeros_ref[...] = jnp.zeros_like(zeros_ref)

        def group_slab(e, carry):
            copy = _slab_copy(zeros_ref, xs_hbm, slab_sem, fill_ref[e])
            copy.start()
            copy.wait()
            return carry

        lax.fori_loop(0, N_EXPERTS, group_slab, 0)
        tail0, n_tail = fill_ref[N_EXPERTS], fill_ref[N_EXPERTS + 1]

        def tail_start(s, carry):
            _slab_copy(zeros_ref, xs_hbm, slab_sem, tail0 + s * MOE_ROWS).start()
            return carry

        def tail_wait(s, carry):
            _slab_copy(zeros_ref, xs_hbm, slab_sem, tail0).wait()
            return carry

        lax.fori_loop(0, n_tail, tail_start, 0)
        lax.fori_loop(0, n_tail, tail_wait, 0)

    def start(t, carry):
        for k in range(TOP_K):
            _token_copy(h_ref, xs_hbm, row_sem, t, dest_ref[0, t * TOP_K + k]).start(priority=k % 2)
        return carry

    def wait(t, carry):
        for k in range(TOP_K):
            _token_copy(h_ref, xs_hbm, row_sem, t, 0).wait()
        return carry

    lax.fori_loop(0, tokens, start, 0, unroll=DMA_UNROLL)
    lax.fori_loop(0, tokens, wait, 0, unroll=DMA_UNROLL)


def dispatch_call(fill, dest, slabs, p_rows):
    tokens = slabs.shape[0]
    steps = tokens // DISPATCH_ROWS
    slab = slabs.shape[1:]
    grid_spec = pltpu.PrefetchScalarGridSpec(
        num_scalar_prefetch=1,
        grid=(steps,),
        in_specs=[pl.BlockSpec((None, 1, DISPATCH_ROWS * TOP_K), lambda i, f: (i, 0, 0), memory_space=pltpu.SMEM),
                  pl.BlockSpec((DISPATCH_ROWS,) + slab, lambda i, f: (i, 0, 0))],
        out_specs=pl.BlockSpec(memory_space=pl.ANY),
        scratch_shapes=[pltpu.VMEM((MOE_ROWS,) + slab, slabs.dtype),
                        pltpu.SemaphoreType.DMA(()), pltpu.SemaphoreType.DMA(())],
    )
    return pl.pallas_call(
        _dispatch_kernel,
        grid_spec=grid_spec,
        out_shape=jax.ShapeDtypeStruct((p_rows + MOE_ROWS,) + slab, slabs.dtype),
        compiler_params=_params(1),
        name="moe_dispatch",
    )(fill, dest.reshape(steps, 1, DISPATCH_ROWS * TOP_K), slabs)


def _tile(i, nv_ref):
    return jnp.maximum(jnp.minimum(i, nv_ref[0] - 1), 0)


def _expert_changed(i, te_ref):
    return jnp.logical_or(i == 0, te_ref[i] != te_ref[jnp.maximum(i - 1, 0)])


def _for_real_rows(i, nv_ref, tr_ref, out_ref, compute):
    used = i < nv_ref[0]
    whole = jnp.logical_and(used, tr_ref[i] > MOE_HALF)

    @pl.when(whole)
    def _():
        out_ref[...] = compute(slice(None))

    @pl.when(jnp.logical_and(used, jnp.logical_not(whole)))
    def _():
        out_ref[:MOE_HALF] = jnp.zeros_like(out_ref[:MOE_HALF])
        out_ref[MOE_HALF:] = compute(slice(MOE_HALF, MOE_ROWS))

    @pl.when(jnp.logical_not(used))
    def _():
        out_ref[...] = jnp.zeros_like(out_ref)


def _moe_up_kernel(te_ref, nv_ref, tr_ref, x_ref, wg_ref, wu_ref, bg_ref, bu_ref, a_ref, wgb_ref, wub_ref):
    i = pl.program_id(1)

    @pl.when(_expert_changed(i, te_ref))
    def _():
        wgb_ref[...] = wg_ref[...].astype(BF16)
        wub_ref[...] = wu_ref[...].astype(BF16)

    def compute(rows):
        x = x_ref[rows].reshape(-1, x_ref.shape[1] * x_ref.shape[2])
        g = jnp.dot(x, wgb_ref[...], preferred_element_type=F32) + bg_ref[...]
        u = jnp.dot(x, wub_ref[...], preferred_element_type=F32) + bu_ref[...]
        g = jnp.minimum(g, SWIGLU_LIMIT)
        u = jnp.clip(u, -SWIGLU_LIMIT, SWIGLU_LIMIT)
        return ((u + 1.0) * g * jax.nn.sigmoid(SWIGLU_ALPHA * g)).astype(a_ref.dtype)

    _for_real_rows(i, nv_ref, tr_ref, a_ref, compute)


def moe_up_call(tile_expert, n_valid, tile_rows, xs, w_up, b_up):
    d = xs.shape[1] * xs.shape[2]
    n_tiles = tile_expert.shape[0]
    p_rows = n_tiles * MOE_ROWS
    n_col = D_EXPERT // UP_COLS
    grid_spec = pltpu.PrefetchScalarGridSpec(
        num_scalar_prefetch=3,
        grid=(n_col, n_tiles),
        in_specs=[
            pl.BlockSpec((MOE_ROWS,) + xs.shape[1:], lambda n, i, te, nv, tr: (_tile(i, nv), 0, 0)),
            pl.BlockSpec((None, d, UP_COLS), lambda n, i, te, nv, tr: (te[_tile(i, nv)], 0, n)),
            pl.BlockSpec((None, d, UP_COLS), lambda n, i, te, nv, tr: (te[_tile(i, nv)], 0, n_col + n)),
            pl.BlockSpec((None, 1, UP_COLS), lambda n, i, te, nv, tr: (te[_tile(i, nv)], 0, n)),
            pl.BlockSpec((None, 1, UP_COLS), lambda n, i, te, nv, tr: (te[_tile(i, nv)], 0, n_col + n)),
        ],
        out_specs=pl.BlockSpec((MOE_ROWS, UP_COLS), lambda n, i, te, nv, tr: (i, n)),
        scratch_shapes=[pltpu.VMEM((d, UP_COLS), BF16), pltpu.VMEM((d, UP_COLS), BF16)],
    )
    return pl.pallas_call(
        _moe_up_kernel,
        grid_spec=grid_spec,
        out_shape=jax.ShapeDtypeStruct((p_rows, D_EXPERT), BF16),
        compiler_params=_params(2),
        name="moe_up",
    )(tile_expert, n_valid, tile_rows, xs, w_up, w_up, b_up, b_up)


def _moe_down_kernel(te_ref, nv_ref, tr_ref, a_ref, w_ref, b_ref, y_ref, wb_ref):
    i = pl.program_id(1)

    @pl.when(_expert_changed(i, te_ref))
    def _():
        wb_ref[...] = w_ref[...].astype(BF16)

    def compute(rows):
        y = jnp.dot(a_ref[rows, :], wb_ref[...], preferred_element_type=F32) + b_ref[...]
        return y.reshape((y.shape[0],) + y_ref.shape[1:]).astype(y_ref.dtype)

    _for_real_rows(i, nv_ref, tr_ref, y_ref, compute)


def moe_down_call(tile_expert, n_valid, tile_rows, act, w_down, b_down):
    p_rows = act.shape[0]
    n_tiles = p_rows // MOE_ROWS
    d_out = w_down.shape[2]
    grid_spec = pltpu.PrefetchScalarGridSpec(
        num_scalar_prefetch=3,
        grid=(d_out // DOWN_COLS, n_tiles),
        in_specs=[
            pl.BlockSpec((MOE_ROWS, D_EXPERT), lambda n, i, te, nv, tr: (_tile(i, nv), 0)),
            pl.BlockSpec((None, D_EXPERT, DOWN_COLS), lambda n, i, te, nv, tr: (te[_tile(i, nv)], 0, n)),
            pl.BlockSpec((None, 1, DOWN_COLS), lambda n, i, te, nv, tr: (te[_tile(i, nv)], 0, n)),
        ],
        out_specs=pl.BlockSpec((MOE_ROWS, DOWN_COLS // LANES, LANES), lambda n, i, te, nv, tr: (i, n, 0)),
        scratch_shapes=[pltpu.VMEM((D_EXPERT, DOWN_COLS), BF16)],
    )
    return pl.pallas_call(
        _moe_down_kernel,
        grid_spec=grid_spec,
        out_shape=jax.ShapeDtypeStruct((p_rows, d_out // LANES, LANES), BF16),
        compiler_params=_params(2),
        name="moe_down",
    )(tile_expert, n_valid, tile_rows, act, w_down, b_down)


def _combine_kernel(dest_ref, next_ref, y_hbm, gate_ref, h_ref, g_ref, b_ref, o_ref, buf_ref, sems):
    step = pl.program_id(0)
    rows = h_ref.shape[0]
    slot = step % 2

    def issue(idx_ref, to_slot):
        def start(r, carry):
            for k in range(TOP_K):
                _token_copy(y_hbm, buf_ref.at[to_slot, k], sems.at[to_slot], idx_ref[0, r * TOP_K + k], r
                            ).start(priority=k % 2)
            return carry
        lax.fori_loop(0, rows, start, 0, unroll=DMA_UNROLL)

    @pl.when(step == 0)
    def _():
        issue(dest_ref, 0)

    @pl.when(step + 1 < pl.num_programs(0))
    def _():
        issue(next_ref, 1 - slot)

    def wait(r, carry):
        for k in range(TOP_K):
            _token_copy(y_hbm, buf_ref.at[slot, k], sems.at[slot], 0, r).wait()
        return carry

    lax.fori_loop(0, rows, wait, 0, unroll=DMA_UNROLL)
    ff = None
    for k in range(TOP_K):
        term = gate_ref[:, k:k + 1] * buf_ref[slot, k].astype(F32).reshape(rows, D_MODEL)
        ff = term if ff is None else ff + term
    o_ref[...] = _layer_norm_rows(DEEPNORM_ALPHA * h_ref[...] + ff, g_ref[...], b_ref[...])


def combine_call(dest, y, gate, h, g, b):
    rows, d = h.shape
    steps = rows // COMBINE_ROWS
    dest3 = dest.reshape(steps, 1, COMBINE_ROWS * TOP_K)
    idx_block = (None, 1, COMBINE_ROWS * TOP_K)
    return pl.pallas_call(
        _combine_kernel,
        grid=(steps,),
        in_specs=[pl.BlockSpec(idx_block, lambda i: (i, 0, 0), memory_space=pltpu.SMEM),
                  pl.BlockSpec(idx_block, lambda i: (jnp.minimum(i + 1, steps - 1), 0, 0), memory_space=pltpu.SMEM),
                  pl.BlockSpec(memory_space=pl.ANY),
                  pl.BlockSpec((COMBINE_ROWS, LANES), lambda i: (i, 0)),
                  pl.BlockSpec((COMBINE_ROWS, d), lambda i: (i, 0)),
                  pl.BlockSpec((1, d), lambda i: (0, 0)),
                  pl.BlockSpec((1, d), lambda i: (0, 0))],
        out_specs=pl.BlockSpec((COMBINE_ROWS, d), lambda i: (i, 0)),
        out_shape=jax.ShapeDtypeStruct((rows, d), F32),
        scratch_shapes=[pltpu.VMEM((2, TOP_K, COMBINE_ROWS, ROW_TILES, LANES), y.dtype),
                        pltpu.SemaphoreType.DMA((2,))],
        compiler_params=_params(1),
        name="moe_combine",
    )(dest3, dest3, y, gate, h, g.reshape(1, d), b.reshape(1, d))


def _pad_lanes(a, rows):
    out = jnp.zeros((rows, LANES), a.dtype)
    return out.at[:, :a.shape[-1]].set(a.reshape(rows, -1))


def kernel(x, mem, ln_in_g, ln_in_b, w_in, b_f, fox_q_norm_g, fox_k_norm_g, mix_norm_g, w_out, ln_mix_g, ln_mix_b, mem_ln_g, mem_ln_b, xa_wq, xa_wkv, xa_wo, ln_xa_g, ln_xa_b, router_w, router_b, w_up, b_up, w_down, b_down, ln_moe_g, ln_moe_b):
    batch, seq, d = x.shape
    n_mem = mem.shape[1]
    tokens = batch * seq
    assert w_in.shape[0] == DEPTH == 1 and d == D_MODEL

    h0, h0b = layer_norm_call(x.reshape(tokens, d), ln_in_g, ln_in_b, ("f32", "bf16"))
    w_in_t = jnp.swapaxes(w_in, 1, 2)[0]
    sb0 = 4 * FOX_W + N_FOX_HEADS
    p_fox = matmul_call([(h0b, w_in_t, 0, 0, True)], 4 * FOX_W, BF16, name="in_proj_fox")
    f_logit = matmul_call([(h0b, w_in_t, 0, 4 * FOX_W, True)], LANES, F32, tn=LANES, name="in_proj_forget")
    p_sb = matmul_call([(h0b, w_in_t, 0, sb0, True)], 3 * SB_W, BF16, name="in_proj_sb")

    c = forget_cumsum_call(f_logit, _pad_lanes(b_f[0], 1), batch, seq)
    c = c.reshape(batch, seq, LANES)[:, :, :N_FOX_HEADS].transpose(0, 2, 1)
    gm = mix_norm_g[0].reshape(-1, 1, HEAD_DIM)
    o_f = fox_attention_call(p_fox, c[..., None], c[:, :, None, :],
                             fox_q_norm_g[0].reshape(1, HEAD_DIM), fox_k_norm_g[0].reshape(1, HEAD_DIM),
                             gm, batch, seq)
    o_s = sb_attention_call(p_sb, gm, batch, seq)
    r1 = matmul_call([(o_f, w_out[0], 0, 0, False), (o_s, w_out[0], FOX_W, 0, False)], d, F32,
                     res=h0, alpha=DEEPNORM_ALPHA, name="out_proj")
    h1, h1b = layer_norm_call(r1, ln_mix_g[0], ln_mix_b[0], ("f32", "bf16"))

    (mem_nb,) = layer_norm_call(mem.reshape(batch * n_mem, d), mem_ln_g[0], mem_ln_b[0], ("bf16",))
    kv = matmul_call([(mem_nb, xa_wkv[0], 0, 0, False)], 2 * XA_W, BF16, name="xa_kv_proj")
    qx = matmul_call([(h1b, xa_wq[0], 0, 0, False)], XA_W, BF16, name="xa_q_proj")
    ox = cross_attention_call(qx, kv, seq, n_mem)
    r2 = matmul_call([(ox, xa_wo[0], 0, 0, False)], d, F32, res=h1, alpha=DEEPNORM_ALPHA, name="xa_out_proj")
    h2, h2s = layer_norm_call(r2, ln_xa_g[0], ln_xa_b[0], ("f32", "slab"))

    idx, gate, rank, counts = router_call(h2, _pad_lanes(router_w[0], d), _pad_lanes(router_b[0], 1))
    counts = counts[0, :N_EXPERTS].astype(jnp.int32)
    padded = (counts + MOE_ROWS - 1) // MOE_ROWS * MOE_ROWS
    ends = jnp.cumsum(padded)
    starts = ends - padded
    first = ends - counts
    dest = (first[idx[:, :TOP_K]] + rank[:, :TOP_K]).reshape(-1)
    p_rows = tokens * TOP_K + N_EXPERTS * MOE_ROWS
    n_tiles = p_rows // MOE_ROWS
    tile_row0 = jnp.arange(n_tiles, dtype=jnp.int32) * MOE_ROWS
    tile_expert = jnp.minimum(jnp.sum(ends[None, :] <= tile_row0[:, None], axis=1), N_EXPERTS - 1).astype(jnp.int32)
    tile_rows = jnp.clip(tile_row0 + MOE_ROWS - first[tile_expert], 0, MOE_ROWS).astype(jnp.int32)
    n_valid = (ends[-1:] // MOE_ROWS).astype(jnp.int32)
    fill = jnp.concatenate([starts, ends[-1:], (p_rows + MOE_ROWS - ends[-1:]) // MOE_ROWS]).astype(jnp.int32)

    xs = dispatch_call(fill, dest, h2s, p_rows)
    act = moe_up_call(tile_expert, n_valid, tile_rows, xs, w_up[0], b_up[0].reshape(N_EXPERTS, 1, -1))
    y = moe_down_call(tile_expert, n_valid, tile_rows, act, w_down[0], b_down[0].reshape(N_EXPERTS, 1, -1))
    out = combine_call(dest, y, gate, h2, ln_moe_g[0], ln_moe_b[0])
    return out.reshape(batch, seq, d)
```

```python
import functools

import jax
import jax.numpy as jnp
from jax import lax
from jax.experimental import pallas as pl
from jax.experimental.pallas import tpu as pltpu

F32 = jnp.float32
BF16 = jnp.bfloat16

D_MODEL = 4096
HEAD_DIM = 128
N_FOX_HEADS = 16
N_SB_HEADS = 16
FOX_W = N_FOX_HEADS * HEAD_DIM
SB_W = N_SB_HEADS * HEAD_DIM
XA_HEADS = 4
XA_W = XA_HEADS * HEAD_DIM
N_EXPERTS = 32
TOP_K = 4
D_EXPERT = D_MODEL // 4
SWIGLU_LIMIT = 7.0
SWIGLU_ALPHA = 1.702
LN_EPS = 1e-5
RMS_EPS = 1e-6
DEPTH = 1
DEEPNORM_ALPHA = (2 * DEPTH) ** 0.25
ATTN_SCALE = HEAD_DIM ** -0.5

LANES = 128
BF16_SUBLANES = 16
V7X_VMEM_BYTES = 64 * 1024 * 1024
VMEM_LIMIT = V7X_VMEM_BYTES - 8 * 1024 * 1024

LN_ROWS = 256
MM_ROWS = 2048
MM_SUB = 1024
MM_COLS = 256
ATT_ROWS = 1024
ATT_KEYS = 256
ATT_HEADS = 2
XA_ROWS = 512
RT_ROWS = 256
MOE_ROWS = 512
MOE_HALF = MOE_ROWS // 2
UP_COLS = 512
DOWN_COLS = 2048
DISPATCH_ROWS = 512
COMBINE_ROWS = 128
DMA_UNROLL = 8

ROW_TILES = D_MODEL // LANES


def _params(n_axes, vmem=VMEM_LIMIT):
    return pltpu.CompilerParams(dimension_semantics=("arbitrary",) * n_axes,
                                vmem_limit_bytes=vmem)


def _layer_norm_rows(xf, g, b):
    mu = jnp.mean(xf, axis=-1, keepdims=True)
    xc = xf - mu
    var = jnp.mean(xc * xc, axis=-1, keepdims=True)
    return xc * lax.rsqrt(var + LN_EPS) * g + b


def _ln_kernel(x_ref, g_ref, b_ref, *out_refs, kinds):
    y = _layer_norm_rows(x_ref[...].astype(F32), g_ref[...], b_ref[...])
    for ref, kind in zip(out_refs, kinds):
        if kind == "f32":
            ref[...] = y
        elif kind == "bf16":
            ref[...] = y.astype(BF16)
        else:
            ref[...] = y.reshape(ref.shape).astype(BF16)


def layer_norm_call(x, g, b, kinds):
    rows, d = x.shape
    out_shape, out_specs = [], []
    for kind in kinds:
        if kind == "slab":
            out_shape.append(jax.ShapeDtypeStruct((rows, ROW_TILES, LANES), BF16))
            out_specs.append(pl.BlockSpec((LN_ROWS, ROW_TILES, LANES), lambda i: (i, 0, 0)))
        else:
            out_shape.append(jax.ShapeDtypeStruct((rows, d), F32 if kind == "f32" else BF16))
            out_specs.append(pl.BlockSpec((LN_ROWS, d), lambda i: (i, 0)))
    return pl.pallas_call(
        functools.partial(_ln_kernel, kinds=kinds),
        grid=(rows // LN_ROWS,),
        in_specs=[pl.BlockSpec((LN_ROWS, d), lambda i: (i, 0)),
                  pl.BlockSpec((1, d), lambda i: (0, 0)),
                  pl.BlockSpec((1, d), lambda i: (0, 0))],
        out_specs=out_specs,
        out_shape=out_shape,
        compiler_params=_params(1),
        name="layer_norm",
    )(x, g.reshape(1, d), b.reshape(1, d))


_NT_DIMS = (((1,), (1,)), ((), ()))


def _mm_kernel(*refs, transposed, has_res, alpha, sub):
    n_pairs = len(transposed)
    lhs_refs = refs[:n_pairs]
    w_refs = refs[n_pairs:2 * n_pairs]
    pos = 2 * n_pairs
    res_ref = refs[pos] if has_res else None
    pos += int(has_res)
    out_ref = refs[pos]
    wb_refs = refs[pos + 1:]
    for w_ref, wb_ref in zip(w_refs, wb_refs):
        wb_ref[...] = w_ref[...].astype(BF16)
    rows = out_ref.shape[0]

    def body(r, carry):
        sl = pl.ds(pl.multiple_of(r * sub, sub), sub)
        acc = None
        for lhs_ref, wb_ref, tr in zip(lhs_refs, wb_refs, transposed):
            if tr:
                part = lax.dot_general(lhs_ref[sl, :], wb_ref[...], _NT_DIMS, preferred_element_type=F32)
            else:
                part = jnp.dot(lhs_ref[sl, :], wb_ref[...], preferred_element_type=F32)
            acc = part if acc is None else acc + part
        if has_res:
            acc = acc + alpha * res_ref[sl, :]
        out_ref[sl, :] = acc.astype(out_ref.dtype)
        return carry

    lax.fori_loop(0, rows // sub, body, 0)


def matmul_call(pairs, n_cols, out_dtype, res=None, alpha=1.0, tn=MM_COLS, tm=MM_ROWS, name="matmul"):
    m = pairs[0][0].shape[0]
    tm = min(tm, m)
    sub = min(MM_SUB, tm)
    in_specs, args, scratch = [], [], []
    for lhs, _, _, _, _ in pairs:
        in_specs.append(pl.BlockSpec((tm, lhs.shape[1]), lambda i, j: (i, 0)))
        args.append(lhs)
    for lhs, w, k0, col0, tr in pairs:
        k = lhs.shape[1]
        if tr:
            in_specs.append(pl.BlockSpec(
                (pl.Element(tn), pl.Element(k)),
                lambda i, j, k0=k0, col0=col0: (pl.multiple_of(col0 + j * tn, BF16_SUBLANES), k0)))
            scratch.append(pltpu.VMEM((tn, k), BF16))
        else:
            in_specs.append(pl.BlockSpec((k, tn), lambda i, j, rb=k0 // k, cb=col0 // tn: (rb, j + cb)))
            scratch.append(pltpu.VMEM((k, tn), BF16))
        args.append(w)
    if res is not None:
        in_specs.append(pl.BlockSpec((tm, tn), lambda i, j: (i, j)))
        args.append(res)
    return pl.pallas_call(
        functools.partial(_mm_kernel, transposed=tuple(p[4] for p in pairs), has_res=res is not None,
                          alpha=alpha, sub=sub),
        grid=(m // tm, n_cols // tn),
        in_specs=in_specs,
        out_specs=pl.BlockSpec((tm, tn), lambda i, j: (i, j)),
        out_shape=jax.ShapeDtypeStruct((m, n_cols), out_dtype),
        scratch_shapes=scratch,
        compiler_params=_params(2),
        name=name,
    )(*args)


def _log_sigmoid(x):
    return jnp.minimum(x, 0.0) - jnp.log1p(jnp.exp(-jnp.abs(x)))


def _split3_bf16(x):
    a = x.astype(BF16)
    r = x - a.astype(F32)
    b = r.astype(BF16)
    c = (r - b.astype(F32)).astype(BF16)
    return a, b, c


def _forget_cumsum_kernel(f_ref, b_ref, c_ref, *, chunk):
    seq = f_ref.shape[0]
    row = lax.broadcasted_iota(jnp.int32, (chunk, chunk), 0)
    col = lax.broadcasted_iota(jnp.int32, (chunk, chunk), 1)
    tri = (col <= row).astype(BF16)
    carry = jnp.zeros((1, f_ref.shape[1]), F32)
    for c0 in range(0, seq, chunk):
        lf = _log_sigmoid(f_ref[c0:c0 + chunk, :] + b_ref[...])
        part = sum(jnp.dot(tri, piece, preferred_element_type=F32) for piece in _split3_bf16(lf))
        cs = part + carry
        c_ref[c0:c0 + chunk, :] = cs
        carry = cs[chunk - 1:chunk, :]


def forget_cumsum_call(f_logit, b_f, batch, seq):
    return pl.pallas_call(
        functools.partial(_forget_cumsum_kernel, chunk=ATT_KEYS),
        grid=(batch,),
        in_specs=[pl.BlockSpec((seq, LANES), lambda b: (b, 0)),
                  pl.BlockSpec((1, LANES), lambda b: (0, 0))],
        out_specs=pl.BlockSpec((seq, LANES), lambda b: (b, 0)),
        out_shape=jax.ShapeDtypeStruct((batch * seq, LANES), F32),
        compiler_params=_params(1),
        name="forget_cumsum",
    )(f_logit, b_f)


def _rms_rows(xf, g):
    return xf * lax.rsqrt(jnp.mean(xf * xf, axis=-1, keepdims=True) + RMS_EPS) * g


def _head_cols(a):
    return slice(a * HEAD_DIM, (a + 1) * HEAD_DIM)


def _store_values_transposed(v_ref, vt_ref):
    for a in range(ATT_HEADS):
        for s0 in range(0, v_ref.shape[0], ATT_KEYS):
            vt_ref[a, :, s0:s0 + ATT_KEYS] = v_ref[s0:s0 + ATT_KEYS, _head_cols(a)].astype(F32).T.astype(BF16)


def _fox_kernel(q_ref, k_ref, v_ref, gate_ref, ckey_ref, cqry_ref, gq_ref, gk_ref, gm_ref,
                o_ref, kn_ref, vt_ref, m_ref, l_ref, acc_ref):
    qi = pl.program_id(2)
    seq = k_ref.shape[0]
    nq = q_ref.shape[0]
    blk = ATT_KEYS
    per = nq // blk
    heads = range(ATT_HEADS)

    @pl.when(qi == 0)
    def _():
        _store_values_transposed(v_ref, vt_ref)
        for a in heads:
            for s0 in range(0, seq, blk):
                kn = _rms_rows(k_ref[s0:s0 + blk, _head_cols(a)].astype(F32), gk_ref[...])
                kn_ref[a, s0:s0 + blk, :] = kn.astype(BF16)

    qt = [(_rms_rows(q_ref[:, _head_cols(a)].astype(F32), gq_ref[...]) * ATTN_SCALE).T.astype(BF16)
          for a in heads]
    c_q = [cqry_ref[a] for a in heads]

    def step(a, j, col0, diagonal):
        cols = slice(col0, nq)
        m_prev = m_ref[a, :, cols]
        ks = pl.ds(pl.multiple_of(j * blk, blk), blk)
        s = jnp.dot(kn_ref[a, ks, :], qt[a][:, cols], preferred_element_type=F32)
        s = s + (c_q[a][:, cols] - ckey_ref[a, ks, :])
        if diagonal:
            key = lax.broadcasted_iota(jnp.int32, s.shape, 0)
            qry = lax.broadcasted_iota(jnp.int32, s.shape, 1)
            s = jnp.where(key <= qry, s, -jnp.inf)
        m_new = jnp.maximum(m_prev, jnp.max(s, axis=0, keepdims=True))
        p = jnp.exp(s - m_new)
        scale = jnp.exp(m_prev - m_new)
        m_ref[a, :, cols] = m_new
        l_ref[a, :, cols] = scale * l_ref[a, :, cols] + jnp.sum(p, axis=0, keepdims=True)
        acc_ref[a, :, cols] = scale * acc_ref[a, :, cols] + jnp.dot(
            vt_ref[a, :, ks], p.astype(BF16), preferred_element_type=F32)

    m_ref[...] = jnp.full(m_ref.shape, -jnp.inf, F32)
    l_ref[...] = jnp.zeros_like(l_ref)
    acc_ref[...] = jnp.zeros_like(acc_ref)

    def full_blocks(it, carry):
        for r in range(per):
            for a in heads:
                step(a, per * it + r, 0, False)
        return carry

    lax.fori_loop(0, qi, full_blocks, 0)
    for r in range(per):
        for a in heads:
            step(a, per * qi + r, r * blk, True)
    for a in heads:
        o = _rms_rows((acc_ref[a] / l_ref[a]).T, gm_ref[a]) * jax.nn.sigmoid(gate_ref[:, _head_cols(a)].astype(F32))
        o_ref[:, _head_cols(a)] = o.astype(o_ref.dtype)


def fox_attention_call(p_fox, c_col, c_row, gq, gk, gm, batch, seq):
    nq = seq // ATT_ROWS
    width = ATT_HEADS * HEAD_DIM
    groups = N_FOX_HEADS // ATT_HEADS
    return pl.pallas_call(
        _fox_kernel,
        grid=(batch, groups, nq),
        in_specs=[
            pl.BlockSpec((ATT_ROWS, width), lambda b, hp, qi: (b * nq + qi, hp)),
            pl.BlockSpec((seq, width), lambda b, hp, qi: (b, groups + hp)),
            pl.BlockSpec((seq, width), lambda b, hp, qi: (b, 2 * groups + hp)),
            pl.BlockSpec((ATT_ROWS, width), lambda b, hp, qi: (b * nq + qi, 3 * groups + hp)),
            pl.BlockSpec((None, ATT_HEADS, seq, 1), lambda b, hp, qi: (b, hp, 0, 0)),
            pl.BlockSpec((None, ATT_HEADS, 1, ATT_ROWS), lambda b, hp, qi: (b, hp, 0, qi)),
            pl.BlockSpec((1, HEAD_DIM), lambda b, hp, qi: (0, 0)),
            pl.BlockSpec((1, HEAD_DIM), lambda b, hp, qi: (0, 0)),
            pl.BlockSpec((ATT_HEADS, 1, HEAD_DIM), lambda b, hp, qi: (hp, 0, 0)),
        ],
        out_specs=pl.BlockSpec((ATT_ROWS, width), lambda b, hp, qi: (b * nq + qi, hp)),
        out_shape=jax.ShapeDtypeStruct((batch * seq, FOX_W), BF16),
        scratch_shapes=[pltpu.VMEM((ATT_HEADS, seq, HEAD_DIM), BF16), pltpu.VMEM((ATT_HEADS, HEAD_DIM, seq), BF16),
                        pltpu.VMEM((ATT_HEADS, 1, ATT_ROWS), F32), pltpu.VMEM((ATT_HEADS, 1, ATT_ROWS), F32),
                        pltpu.VMEM((ATT_HEADS, HEAD_DIM, ATT_ROWS), F32)],
        compiler_params=_params(3),
        name="fox_attention",
    )(p_fox, p_fox, p_fox, p_fox, c_col, c_row, gq, gk, gm)


def _sb_kernel(q_ref, k_ref, v_ref, gm_ref, o_ref, vt_ref, after_ref, later_ref, acc_ref):
    qi = pl.program_id(2)
    nq = q_ref.shape[0]
    blk = ATT_KEYS
    per = nq // blk
    heads = range(ATT_HEADS)

    @pl.when(qi == 0)
    def _():
        _store_values_transposed(v_ref, vt_ref)
        row = lax.broadcasted_iota(jnp.int32, (blk, blk), 0)
        col = lax.broadcasted_iota(jnp.int32, (blk, blk), 1)
        after_ref[...] = (col > row).astype(BF16)

    qt = [(q_ref[:, _head_cols(a)].astype(F32) * ATTN_SCALE).T.astype(BF16) for a in heads]

    def step(a, j, col0, diagonal):
        cols = slice(col0, nq)
        ks = pl.ds(pl.multiple_of(j * blk, blk), blk)
        z = jnp.dot(k_ref[ks, _head_cols(a)], qt[a][:, cols], preferred_element_type=F32)
        softplus = jnp.maximum(z, 0.0) + jnp.log(1.0 + jnp.exp(-jnp.abs(z)))
        log_1m = -softplus
        if diagonal:
            key = lax.broadcasted_iota(jnp.int32, z.shape, 0)
            qry = lax.broadcasted_iota(jnp.int32, z.shape, 1)
            strict = key < qry
            log_1m = jnp.where(strict, log_1m, 0.0)
        hi = log_1m.astype(BF16)
        lo = (log_1m - hi.astype(F32)).astype(BF16)
        later = later_ref[a, :, cols]
        tail = (jnp.dot(after_ref[...], hi, preferred_element_type=F32)
                + jnp.dot(after_ref[...], lo, preferred_element_type=F32)) + later
        w = jnp.exp((z - softplus) + tail)
        if diagonal:
            w = jnp.where(strict, w, 0.0)
        acc_ref[a, :, cols] += jnp.dot(vt_ref[a, :, ks], w.astype(BF16), preferred_element_type=F32)
        later_ref[a, :, cols] = later + jnp.sum(log_1m, axis=0, keepdims=True)

    later_ref[...] = jnp.zeros_like(later_ref)
    acc_ref[...] = jnp.zeros_like(acc_ref)
    for r in reversed(range(per)):
        for a in heads:
            step(a, per * qi + r, r * blk, True)

    def full_blocks(it, carry):
        for r in range(per):
            for a in heads:
                step(a, per * (qi - it) - 1 - r, 0, False)
        return carry

    lax.fori_loop(0, qi, full_blocks, 0)
    for a in heads:
        o_ref[:, _head_cols(a)] = _rms_rows(acc_ref[a].T, gm_ref[a]).astype(o_ref.dtype)


def sb_attention_call(p_sb, gm, batch, seq):
    nq = seq // ATT_ROWS
    width = ATT_HEADS * HEAD_DIM
    groups = N_SB_HEADS // ATT_HEADS
    return pl.pallas_call(
        _sb_kernel,
        grid=(batch, groups, nq),
        in_specs=[
            pl.BlockSpec((ATT_ROWS, width), lambda b, hp, qi: (b * nq + qi, hp)),
            pl.BlockSpec((seq, width), lambda b, hp, qi: (b, groups + hp)),
            pl.BlockSpec((seq, width), lambda b, hp, qi: (b, 2 * groups + hp)),
            pl.BlockSpec((ATT_HEADS, 1, HEAD_DIM), lambda b, hp, qi: (N_FOX_HEADS // ATT_HEADS + hp, 0, 0)),
        ],
        out_specs=pl.BlockSpec((ATT_ROWS, width), lambda b, hp, qi: (b * nq + qi, hp)),
        out_shape=jax.ShapeDtypeStruct((batch * seq, SB_W), BF16),
        scratch_shapes=[pltpu.VMEM((ATT_HEADS, HEAD_DIM, seq), BF16), pltpu.VMEM((ATT_KEYS, ATT_KEYS), BF16),
                        pltpu.VMEM((ATT_HEADS, 1, ATT_ROWS), F32), pltpu.VMEM((ATT_HEADS, HEAD_DIM, ATT_ROWS), F32)],
        compiler_params=_params(3),
        name="sb_attention",
    )(p_sb, p_sb, p_sb, gm)


def _xattn_kernel(q_ref, kv_ref, o_ref):
    for hd in range(XA_HEADS):
        cols = _head_cols(hd)
        q = q_ref[:, cols]
        k = kv_ref[:, cols]
        v = kv_ref[:, XA_W + hd * HEAD_DIM:XA_W + (hd + 1) * HEAD_DIM]
        s = lax.dot_general(q, k, _NT_DIMS, preferred_element_type=F32) * ATTN_SCALE
        p = jnp.exp(s - jnp.max(s, axis=-1, keepdims=True))
        p = p / jnp.sum(p, axis=-1, keepdims=True)
        o_ref[:, cols] = jnp.dot(p.astype(BF16), v, preferred_element_type=F32).astype(o_ref.dtype)


def cross_attention_call(q, kv, seq, n_mem):
    rows = q.shape[0]
    per_batch = seq // XA_ROWS
    return pl.pallas_call(
        _xattn_kernel,
        grid=(rows // XA_ROWS,),
        in_specs=[pl.BlockSpec((XA_ROWS, XA_W), lambda i: (i, 0)),
                  pl.BlockSpec((n_mem, 2 * XA_W), lambda i: (i // per_batch, 0))],
        out_specs=pl.BlockSpec((XA_ROWS, XA_W), lambda i: (i, 0)),
        out_shape=jax.ShapeDtypeStruct((rows, XA_W), BF16),
        compiler_params=_params(1),
        name="cross_attention",
    )(q, kv)


def _router_kernel(h_ref, w_ref, b_ref, idx_ref, gate_ref, rank_ref, count_ref, run_ref):
    step = pl.program_id(0)
    rows = h_ref.shape[0]

    @pl.when(step == 0)
    def _():
        run_ref[...] = jnp.zeros_like(run_ref)

    logits = jnp.dot(h_ref[...], w_ref[...], preferred_element_type=F32,
                     precision=lax.Precision.HIGHEST) + b_ref[...]
    lane = lax.broadcasted_iota(jnp.int32, (rows, LANES), 1).astype(F32)
    work = jnp.where(lane < N_EXPERTS, logits, -jnp.inf)
    picks, values = [], []
    for _ in range(TOP_K):
        best = jnp.max(work, axis=-1, keepdims=True)
        first = jnp.min(jnp.where(work == best, lane, float(LANES)), axis=-1, keepdims=True)
        hit = lane == first
        picks.append((first, hit))
        values.append(best)
        work = jnp.where(hit, -jnp.inf, work)
    expv = [jnp.exp(v - values[0]) for v in values]
    denom = sum(expv)

    chosen = sum(hit.astype(F32) for _, hit in picks)
    r = lax.broadcasted_iota(jnp.int32, (rows, rows), 0)
    c = lax.broadcasted_iota(jnp.int32, (rows, rows), 1)
    before = (c < r).astype(BF16)
    prefix = jnp.dot(before, chosen.astype(BF16), preferred_element_type=F32) + run_ref[...]

    idx_out = jnp.zeros((rows, LANES), jnp.int32)
    gate_out = jnp.zeros((rows, LANES), F32)
    rank_out = jnp.zeros((rows, LANES), jnp.int32)
    for k, (first, hit) in enumerate(picks):
        rank_k = jnp.sum(jnp.where(hit, prefix, 0.0), axis=-1, keepdims=True).astype(jnp.int32)
        idx_out = jnp.where(lane == k, first.astype(jnp.int32), idx_out)
        gate_out = jnp.where(lane == k, expv[k] / denom, gate_out)
        rank_out = jnp.where(lane == k, rank_k, rank_out)
    idx_ref[...] = idx_out
    gate_ref[...] = gate_out
    rank_ref[...] = rank_out
    run_ref[...] += jnp.sum(chosen, axis=0, keepdims=True)
    count_ref[...] = run_ref[...]


def router_call(h, w_pad, b_pad):
    rows, d = h.shape
    tok = pl.BlockSpec((RT_ROWS, LANES), lambda i: (i, 0))
    return pl.pallas_call(
        _router_kernel,
        grid=(rows // RT_ROWS,),
        in_specs=[pl.BlockSpec((RT_ROWS, d), lambda i: (i, 0)),
                  pl.BlockSpec((d, LANES), lambda i: (0, 0)),
                  pl.BlockSpec((1, LANES), lambda i: (0, 0))],
        out_specs=[tok, tok, tok, pl.BlockSpec((1, LANES), lambda i: (0, 0))],
        out_shape=[jax.ShapeDtypeStruct((rows, LANES), jnp.int32),
                   jax.ShapeDtypeStruct((rows, LANES), F32),
                   jax.ShapeDtypeStruct((rows, LANES), jnp.int32),
                   jax.ShapeDtypeStruct((1, LANES), F32)],
        scratch_shapes=[pltpu.VMEM((1, LANES), F32)],
        compiler_params=_params(1),
        name="router",
    )(h, w_pad, b_pad)


def _slab_copy(zeros_ref, dst_hbm, sem, row):
    return pltpu.make_async_copy(zeros_ref, dst_hbm.at[pl.ds(row, MOE_ROWS)], sem)


def _token_copy(src_ref, dst_ref, sem, src_row, dst_row):
    return pltpu.make_async_copy(src_ref.at[pl.ds(src_row, 1)], dst_ref.at[pl.ds(dst_row, 1)], sem)


def _dispatch_kernel(fill_ref, dest_ref, h_ref, xs_hbm, zeros_ref, slab_sem, row_sem):
    tokens = h_ref.shape[0]

    @pl.when(pl.program_id(0) == 0)
    def _():
        zeros_ref[...] = jnp.zeros_like(zeros_ref)

        def group_slab(e, carry):
            copy = _slab_copy(zeros_ref, xs_hbm, slab_sem, fill_ref[e])
            copy.start()
            copy.wait()
            return carry

        lax.fori_loop(0, N_EXPERTS, group_slab, 0)
        tail0, n_tail = fill_ref[N_EXPERTS], fill_ref[N_EXPERTS + 1]

        def tail_start(s, carry):
            _slab_copy(zeros_ref, xs_hbm, slab_sem, tail0 + s * MOE_ROWS).start()
            return carry

        def tail_wait(s, carry):
            _slab_copy(zeros_ref, xs_hbm, slab_sem, tail0).wait()
            return carry

        lax.fori_loop(0, n_tail, tail_start, 0)
        lax.fori_loop(0, n_tail, tail_wait, 0)

    def start(t, carry):
        for k in range(TOP_K):
            _token_copy(h_ref, xs_hbm, row_sem, t, dest_ref[0, t * TOP_K + k]).start(priority=k % 2)
        return carry

    def wait(t, carry):
        for k in range(TOP_K):
            _token_copy(h_ref, xs_hbm, row_sem, t, 0).wait()
        return carry

    lax.fori_loop(0, tokens, start, 0, unroll=DMA_UNROLL)
    lax.fori_loop(0, tokens, wait, 0, unroll=DMA_UNROLL)


def dispatch_call(fill, dest, slabs, p_rows):
    tokens = slabs.shape[0]
    steps = tokens // DISPATCH_ROWS
    slab = slabs.shape[1:]
    grid_spec = pltpu.PrefetchScalarGridSpec(
        num_scalar_prefetch=1,
        grid=(steps,),
        in_specs=[pl.BlockSpec((None, 1, DISPATCH_ROWS * TOP_K), lambda i, f: (i, 0, 0), memory_space=pltpu.SMEM),
                  pl.BlockSpec((DISPATCH_ROWS,) + slab, lambda i, f: (i, 0, 0))],
        out_specs=pl.BlockSpec(memory_space=pl.ANY),
        scratch_shapes=[pltpu.VMEM((MOE_ROWS,) + slab, slabs.dtype),
                        pltpu.SemaphoreType.DMA(()), pltpu.SemaphoreType.DMA(())],
    )
    return pl.pallas_call(
        _dispatch_kernel,
        grid_spec=grid_spec,
        out_shape=jax.ShapeDtypeStruct((p_rows + MOE_ROWS,) + slab, slabs.dtype),
        compiler_params=_params(1),
        name="moe_dispatch",
    )(fill, dest.reshape(steps, 1, DISPATCH_ROWS * TOP_K), slabs)


def _tile(i, nv_ref):
    return jnp.maximum(jnp.minimum(i, nv_ref[0] - 1), 0)


def _expert_changed(i, te_ref):
    return jnp.logical_or(i == 0, te_ref[i] != te_ref[jnp.maximum(i - 1, 0)])


def _for_real_rows(i, nv_ref, tr_ref, out_ref, compute):
    used = i < nv_ref[0]
    whole = jnp.logical_and(used, tr_ref[i] > MOE_HALF)

    @pl.when(whole)
    def _():
        out_ref[...] = compute(slice(None))

    @pl.when(jnp.logical_and(used, jnp.logical_not(whole)))
    def _():
        out_ref[:MOE_HALF] = jnp.zeros_like(out_ref[:MOE_HALF])
        out_ref[MOE_HALF:] = compute(slice(MOE_HALF, MOE_ROWS))

    @pl.when(jnp.logical_not(used))
    def _():
        out_ref[...] = jnp.zeros_like(out_ref)


def _moe_up_kernel(te_ref, nv_ref, tr_ref, x_ref, wg_ref, wu_ref, bg_ref, bu_ref, a_ref, wgb_ref, wub_ref):
    i = pl.program_id(1)

    @pl.when(_expert_changed(i, te_ref))
    def _():
        wgb_ref[...] = wg_ref[...].astype(BF16)
        wub_ref[...] = wu_ref[...].astype(BF16)

    def compute(rows):
        x = x_ref[rows].reshape(-1, x_ref.shape[1] * x_ref.shape[2])
        g = jnp.dot(x, wgb_ref[...], preferred_element_type=F32) + bg_ref[...]
        u = jnp.dot(x, wub_ref[...], preferred_element_type=F32) + bu_ref[...]
        g = jnp.minimum(g, SWIGLU_LIMIT)
        u = jnp.clip(u, -SWIGLU_LIMIT, SWIGLU_LIMIT)
        return ((u + 1.0) * g * jax.nn.sigmoid(SWIGLU_ALPHA * g)).astype(a_ref.dtype)

    _for_real_rows(i, nv_ref, tr_ref, a_ref, compute)


def moe_up_call(tile_expert, n_valid, tile_rows, xs, w_up, b_up):
    d = xs.shape[1] * xs.shape[2]
    n_tiles = tile_expert.shape[0]
    p_rows = n_tiles * MOE_ROWS
    n_col = D_EXPERT // UP_COLS
    grid_spec = pltpu.PrefetchScalarGridSpec(
        num_scalar_prefetch=3,
        grid=(n_col, n_tiles),
        in_specs=[
            pl.BlockSpec((MOE_ROWS,) + xs.shape[1:], lambda n, i, te, nv, tr: (_tile(i, nv), 0, 0)),
            pl.BlockSpec((None, d, UP_COLS), lambda n, i, te, nv, tr: (te[_tile(i, nv)], 0, n)),
            pl.BlockSpec((None, d, UP_COLS), lambda n, i, te, nv, tr: (te[_tile(i, nv)], 0, n_col + n)),
            pl.BlockSpec((None, 1, UP_COLS), lambda n, i, te, nv, tr: (te[_tile(i, nv)], 0, n)),
            pl.BlockSpec((None, 1, UP_COLS), lambda n, i, te, nv, tr: (te[_tile(i, nv)], 0, n_col + n)),
        ],
        out_specs=pl.BlockSpec((MOE_ROWS, UP_COLS), lambda n, i, te, nv, tr: (i, n)),
        scratch_shapes=[pltpu.VMEM((d, UP_COLS), BF16), pltpu.VMEM((d, UP_COLS), BF16)],
    )
    return pl.pallas_call(
        _moe_up_kernel,
        grid_spec=grid_spec,
        out_shape=jax.ShapeDtypeStruct((p_rows, D_EXPERT), BF16),
        compiler_params=_params(2),
        name="moe_up",
    )(tile_expert, n_valid, tile_rows, xs, w_up, w_up, b_up, b_up)


def _moe_down_kernel(te_ref, nv_ref, tr_ref, a_ref, w_ref, b_ref, y_ref, wb_ref):
    i = pl.program_id(1)

    @pl.when(_expert_changed(i, te_ref))
    def _():
        wb_ref[...] = w_ref[...].astype(BF16)

    def compute(rows):
        y = jnp.dot(a_ref[rows, :], wb_ref[...], preferred_element_type=F32) + b_ref[...]
        return y.reshape((y.shape[0],) + y_ref.shape[1:]).astype(y_ref.dtype)

    _for_real_rows(i, nv_ref, tr_ref, y_ref, compute)


def moe_down_call(tile_expert, n_valid, tile_rows, act, w_down, b_down):
    p_rows = act.shape[0]
    n_tiles = p_rows // MOE_ROWS
    d_out = w_down.shape[2]
    grid_spec = pltpu.PrefetchScalarGridSpec(
        num_scalar_prefetch=3,
        grid=(d_out // DOWN_COLS, n_tiles),
        in_specs=[
            pl.BlockSpec((MOE_ROWS, D_EXPERT), lambda n, i, te, nv, tr: (_tile(i, nv), 0)),
            pl.BlockSpec((None, D_EXPERT, DOWN_COLS), lambda n, i, te, nv, tr: (te[_tile(i, nv)], 0, n)),
            pl.BlockSpec((None, 1, DOWN_COLS), lambda n, i, te, nv, tr: (te[_tile(i, nv)], 0, n)),
        ],
        out_specs=pl.BlockSpec((MOE_ROWS, DOWN_COLS // LANES, LANES), lambda n, i, te, nv, tr: (i, n, 0)),
        scratch_shapes=[pltpu.VMEM((D_EXPERT, DOWN_COLS), BF16)],
    )
    return pl.pallas_call(
        _moe_down_kernel,
        grid_spec=grid_spec,
        out_shape=jax.ShapeDtypeStruct((p_rows, d_out // LANES, LANES), BF16),
        compiler_params=_params(2),
        name="moe_down",
    )(tile_expert, n_valid, tile_rows, act, w_down, b_down)


def _combine_kernel(dest_ref, next_ref, y_hbm, gate_ref, h_ref, g_ref, b_ref, o_ref, buf_ref, sems):
    step = pl.program_id(0)
    rows = h_ref.shape[0]
    slot = step % 2

    def issue(idx_ref, to_slot):
        def start(r, carry):
            for k in range(TOP_K):
                _token_copy(y_hbm, buf_ref.at[to_slot, k], sems.at[to_slot], idx_ref[0, r * TOP_K + k], r
                            ).start(priority=k % 2)
            return carry
        lax.fori_loop(0, rows, start, 0, unroll=DMA_UNROLL)

    @pl.when(step == 0)
    def _():
        issue(dest_ref, 0)

    @pl.when(step + 1 < pl.num_programs(0))
    def _():
        issue(next_ref, 1 - slot)

    def wait(r, carry):
        for k in range(TOP_K):
            _token_copy(y_hbm, buf_ref.at[slot, k], sems.at[slot], 0, r).wait()
        return carry

    lax.fori_loop(0, rows, wait, 0, unroll=DMA_UNROLL)
    ff = None
    for k in range(TOP_K):
        term = gate_ref[:, k:k + 1] * buf_ref[slot, k].astype(F32).reshape(rows, D_MODEL)
        ff = term if ff is None else ff + term
    o_ref[...] = _layer_norm_rows(DEEPNORM_ALPHA * h_ref[...] + ff, g_ref[...], b_ref[...])


def combine_call(dest, y, gate, h, g, b):
    rows, d = h.shape
    steps = rows // COMBINE_ROWS
    dest3 = dest.reshape(steps, 1, COMBINE_ROWS * TOP_K)
    idx_block = (None, 1, COMBINE_ROWS * TOP_K)
    return pl.pallas_call(
        _combine_kernel,
        grid=(steps,),
        in_specs=[pl.BlockSpec(idx_block, lambda i: (i, 0, 0), memory_space=pltpu.SMEM),
                  pl.BlockSpec(idx_block, lambda i: (jnp.minimum(i + 1, steps - 1), 0, 0), memory_space=pltpu.SMEM),
                  pl.BlockSpec(memory_space=pl.ANY),
                  pl.BlockSpec((COMBINE_ROWS, LANES), lambda i: (i, 0)),
                  pl.BlockSpec((COMBINE_ROWS, d), lambda i: (i, 0)),
                  pl.BlockSpec((1, d), lambda i: (0, 0)),
                  pl.BlockSpec((1, d), lambda i: (0, 0))],
        out_specs=pl.BlockSpec((COMBINE_ROWS, d), lambda i: (i, 0)),
        out_shape=jax.ShapeDtypeStruct((rows, d), F32),
        scratch_shapes=[pltpu.VMEM((2, TOP_K, COMBINE_ROWS, ROW_TILES, LANES), y.dtype),
                        pltpu.SemaphoreType.DMA((2,))],
        compiler_params=_params(1),
        name="moe_combine",
    )(dest3, dest3, y, gate, h, g.reshape(1, d), b.reshape(1, d))


def _pad_lanes(a, rows):
    out = jnp.zeros((rows, LANES), a.dtype)
    return out.at[:, :a.shape[-1]].set(a.reshape(rows, -1))


def kernel(x, mem, ln_in_g, ln_in_b, w_in, b_f, fox_q_norm_g, fox_k_norm_g, mix_norm_g, w_out, ln_mix_g, ln_mix_b, mem_ln_g, mem_ln_b, xa_wq, xa_wkv, xa_wo, ln_xa_g, ln_xa_b, router_w, router_b, w_up, b_up, w_down, b_down, ln_moe_g, ln_moe_b):
    batch, seq, d = x.shape
    n_mem = mem.shape[1]
    tokens = batch * seq
    assert w_in.shape[0] == DEPTH == 1 and d == D_MODEL

    h0, h0b = layer_norm_call(x.reshape(tokens, d), ln_in_g, ln_in_b, ("f32", "bf16"))
    w_in_t = jnp.swapaxes(w_in, 1, 2)[0]
    sb0 = 4 * FOX_W + N_FOX_HEADS
    p_fox = matmul_call([(h0b, w_in_t, 0, 0, True)], 4 * FOX_W, BF16, name="in_proj_fox")
    f_logit = matmul_call([(h0b, w_in_t, 0, 4 * FOX_W, True)], LANES, F32, tn=LANES, name="in_proj_forget")
    p_sb = matmul_call([(h0b, w_in_t, 0, sb0, True)], 3 * SB_W, BF16, name="in_proj_sb")

    c = forget_cumsum_call(f_logit, _pad_lanes(b_f[0], 1), batch, seq)
    c = c.reshape(batch, seq, LANES)[:, :, :N_FOX_HEADS].transpose(0, 2, 1)
    gm = mix_norm_g[0].reshape(-1, 1, HEAD_DIM)
    o_f = fox_attention_call(p_fox, c[..., None], c[:, :, None, :],
                             fox_q_norm_g[0].reshape(1, HEAD_DIM), fox_k_norm_g[0].reshape(1, HEAD_DIM),
                             gm, batch, seq)
    o_s = sb_attention_call(p_sb, gm, batch, seq)
    r1 = matmul_call([(o_f, w_out[0], 0, 0, False), (o_s, w_out[0], FOX_W, 0, False)], d, F32,
                     res=h0, alpha=DEEPNORM_ALPHA, name="out_proj")
    h1, h1b = layer_norm_call(r1, ln_mix_g[0], ln_mix_b[0], ("f32", "bf16"))

    (mem_nb,) = layer_norm_call(mem.reshape(batch * n_mem, d), mem_ln_g[0], mem_ln_b[0], ("bf16",))
    kv = matmul_call([(mem_nb, xa_wkv[0], 0, 0, False)], 2 * XA_W, BF16, name="xa_kv_proj")
    qx = matmul_call([(h1b, xa_wq[0], 0, 0, False)], XA_W, BF16, name="xa_q_proj")
    ox = cross_attention_call(qx, kv, seq, n_mem)
    r2 = matmul_call([(ox, xa_wo[0], 0, 0, False)], d, F32, res=h1, alpha=DEEPNORM_ALPHA, name="xa_out_proj")
    h2, h2s = layer_norm_call(r2, ln_xa_g[0], ln_xa_b[0], ("f32", "slab"))

    idx, gate, rank, counts = router_call(h2, _pad_lanes(router_w[0], d), _pad_lanes(router_b[0], 1))
    counts = counts[0, :N_EXPERTS].astype(jnp.int32)
    padded = (counts + MOE_ROWS - 1) // MOE_ROWS * MOE_ROWS
    ends = jnp.cumsum(padded)
    starts = ends - padded
    first = ends - counts
    dest = (first[idx[:, :TOP_K]] + rank[:, :TOP_K]).reshape(-1)
    p_rows = tokens * TOP_K + N_EXPERTS * MOE_ROWS
    n_tiles = p_rows // MOE_ROWS
    tile_row0 = jnp.arange(n_tiles, dtype=jnp.int32) * MOE_ROWS
    tile_expert = jnp.minimum(jnp.sum(ends[None, :] <= tile_row0[:, None], axis=1), N_EXPERTS - 1).astype(jnp.int32)
    tile_rows = jnp.clip(tile_row0 + MOE_ROWS - first[tile_expert], 0, MOE_ROWS).astype(jnp.int32)
    n_valid = (ends[-1:] // MOE_ROWS).astype(jnp.int32)
    fill = jnp.concatenate([starts, ends[-1:], (p_rows + MOE_ROWS - ends[-1:]) // MOE_ROWS]).astype(jnp.int32)

    xs = dispatch_call(fill, dest, h2s, p_rows)
    act = moe_up_call(tile_expert, n_valid, tile_rows, xs, w_up[0], b_up[0].reshape(N_EXPERTS, 1, -1))
    y = moe_down_call(tile_expert, n_valid, tile_rows, act, w_down[0], b_down[0].reshape(N_EXPERTS, 1, -1))
    out = combine_call(dest, y, gate, h2, ln_moe_g[0], ln_moe_b[0])
    return out.reshape(batch, seq, d)
```

```python
import functools

import jax
import jax.numpy as jnp
from jax import lax
from jax.experimental import pallas as pl
from jax.experimental.pallas import tpu as pltpu

F32 = jnp.float32
BF16 = jnp.bfloat16

D_MODEL = 4096
HEAD_DIM = 128
N_FOX_HEADS = 16
N_SB_HEADS = 16
FOX_W = N_FOX_HEADS * HEAD_DIM
SB_W = N_SB_HEADS * HEAD_DIM
XA_HEADS = 4
XA_W = XA_HEADS * HEAD_DIM
N_EXPERTS = 32
TOP_K = 4
D_EXPERT = D_MODEL // 4
SWIGLU_LIMIT = 7.0
SWIGLU_ALPHA = 1.702
LN_EPS = 1e-5
RMS_EPS = 1e-6
DEPTH = 1
DEEPNORM_ALPHA = (2 * DEPTH) ** 0.25
ATTN_SCALE = HEAD_DIM ** -0.5

LANES = 128
BF16_SUBLANES = 16
V7X_VMEM_BYTES = 64 * 1024 * 1024
VMEM_LIMIT = V7X_VMEM_BYTES - 8 * 1024 * 1024

LN_ROWS = 256
MM_ROWS = 2048
MM_SUB = 1024
MM_COLS = 256
ATT_ROWS = 2048
ATT_KEYS = 256
ATT_HEADS = 2
XA_ROWS = 512
RT_ROWS = 256
MOE_ROWS = 512
MOE_HALF = MOE_ROWS // 2
UP_COLS = 512
DOWN_COLS = 2048
DISPATCH_ROWS = 512
COMBINE_ROWS = 256
DMA_UNROLL = 8

ROW_TILES = D_MODEL // LANES


def _params(n_axes, vmem=VMEM_LIMIT):
    return pltpu.CompilerParams(dimension_semantics=("arbitrary",) * n_axes,
                                vmem_limit_bytes=vmem)


def _layer_norm_rows(xf, g, b):
    mu = jnp.mean(xf, axis=-1, keepdims=True)
    xc = xf - mu
    var = jnp.mean(xc * xc, axis=-1, keepdims=True)
    return xc * lax.rsqrt(var + LN_EPS) * g + b


def _ln_kernel(x_ref, g_ref, b_ref, *out_refs, kinds):
    y = _layer_norm_rows(x_ref[...].astype(F32), g_ref[...], b_ref[...])
    for ref, kind in zip(out_refs, kinds):
        if kind == "f32":
            ref[...] = y
        elif kind == "bf16":
            ref[...] = y.astype(BF16)
        else:
            ref[...] = y.reshape(ref.shape).astype(BF16)


def layer_norm_call(x, g, b, kinds):
    rows, d = x.shape
    out_shape, out_specs = [], []
    for kind in kinds:
        if kind == "slab":
            out_shape.append(jax.ShapeDtypeStruct((rows, ROW_TILES, LANES), BF16))
            out_specs.append(pl.BlockSpec((LN_ROWS, ROW_TILES, LANES), lambda i: (i, 0, 0)))
        else:
            out_shape.append(jax.ShapeDtypeStruct((rows, d), F32 if kind == "f32" else BF16))
            out_specs.append(pl.BlockSpec((LN_ROWS, d), lambda i: (i, 0)))
    return pl.pallas_call(
        functools.partial(_ln_kernel, kinds=kinds),
        grid=(rows // LN_ROWS,),
        in_specs=[pl.BlockSpec((LN_ROWS, d), lambda i: (i, 0)),
                  pl.BlockSpec((1, d), lambda i: (0, 0)),
                  pl.BlockSpec((1, d), lambda i: (0, 0))],
        out_specs=out_specs,
        out_shape=out_shape,
        compiler_params=_params(1),
        name="layer_norm",
    )(x, g.reshape(1, d), b.reshape(1, d))


_NT_DIMS = (((1,), (1,)), ((), ()))


def _mm_kernel(*refs, transposed, has_res, alpha, sub):
    n_pairs = len(transposed)
    lhs_refs = refs[:n_pairs]
    w_refs = refs[n_pairs:2 * n_pairs]
    pos = 2 * n_pairs
    res_ref = refs[pos] if has_res else None
    pos += int(has_res)
    out_ref = refs[pos]
    wb_refs = refs[pos + 1:]
    for w_ref, wb_ref in zip(w_refs, wb_refs):
        wb_ref[...] = w_ref[...].astype(BF16)
    rows = out_ref.shape[0]

    def body(r, carry):
        sl = pl.ds(pl.multiple_of(r * sub, sub), sub)
        acc = None
        for lhs_ref, wb_ref, tr in zip(lhs_refs, wb_refs, transposed):
            if tr:
                part = lax.dot_general(lhs_ref[sl, :], wb_ref[...], _NT_DIMS, preferred_element_type=F32)
            else:
                part = jnp.dot(lhs_ref[sl, :], wb_ref[...], preferred_element_type=F32)
            acc = part if acc is None else acc + part
        if has_res:
            acc = acc + alpha * res_ref[sl, :]
        out_ref[sl, :] = acc.astype(out_ref.dtype)
        return carry

    lax.fori_loop(0, rows // sub, body, 0)


def matmul_call(pairs, n_cols, out_dtype, res=None, alpha=1.0, tn=MM_COLS, tm=MM_ROWS, name="matmul"):
    m = pairs[0][0].shape[0]
    tm = min(tm, m)
    sub = min(MM_SUB, tm)
    in_specs, args, scratch = [], [], []
    for lhs, _, _, _, _ in pairs:
        in_specs.append(pl.BlockSpec((tm, lhs.shape[1]), lambda i, j: (i, 0)))
        args.append(lhs)
    for lhs, w, k0, col0, tr in pairs:
        k = lhs.shape[1]
        if tr:
            in_specs.append(pl.BlockSpec(
                (pl.Element(tn), pl.Element(k)),
                lambda i, j, k0=k0, col0=col0: (pl.multiple_of(col0 + j * tn, BF16_SUBLANES), k0)))
            scratch.append(pltpu.VMEM((tn, k), BF16))
        else:
            in_specs.append(pl.BlockSpec((k, tn), lambda i, j, rb=k0 // k, cb=col0 // tn: (rb, j + cb)))
            scratch.append(pltpu.VMEM((k, tn), BF16))
        args.append(w)
    if res is not None:
        in_specs.append(pl.BlockSpec((tm, tn), lambda i, j: (i, j)))
        args.append(res)
    return pl.pallas_call(
        functools.partial(_mm_kernel, transposed=tuple(p[4] for p in pairs), has_res=res is not None,
                          alpha=alpha, sub=sub),
        grid=(m // tm, n_cols // tn),
        in_specs=in_specs,
        out_specs=pl.BlockSpec((tm, tn), lambda i, j: (i, j)),
        out_shape=jax.ShapeDtypeStruct((m, n_cols), out_dtype),
        scratch_shapes=scratch,
        compiler_params=_params(2),
        name=name,
    )(*args)


def _log_sigmoid(x):
    return jnp.minimum(x, 0.0) - jnp.log1p(jnp.exp(-jnp.abs(x)))


def _split3_bf16(x):
    a = x.astype(BF16)
    r = x - a.astype(F32)
    b = r.astype(BF16)
    c = (r - b.astype(F32)).astype(BF16)
    return a, b, c


def _forget_cumsum_kernel(f_ref, b_ref, c_ref, *, chunk):
    seq = f_ref.shape[0]
    row = lax.broadcasted_iota(jnp.int32, (chunk, chunk), 0)
    col = lax.broadcasted_iota(jnp.int32, (chunk, chunk), 1)
    tri = (col <= row).astype(BF16)
    carry = jnp.zeros((1, f_ref.shape[1]), F32)
    for c0 in range(0, seq, chunk):
        lf = _log_sigmoid(f_ref[c0:c0 + chunk, :] + b_ref[...])
        part = sum(jnp.dot(tri, piece, preferred_element_type=F32) for piece in _split3_bf16(lf))
        cs = part + carry
        c_ref[c0:c0 + chunk, :] = cs
        carry = cs[chunk - 1:chunk, :]


def forget_cumsum_call(f_logit, b_f, batch, seq):
    return pl.pallas_call(
        functools.partial(_forget_cumsum_kernel, chunk=ATT_KEYS),
        grid=(batch,),
        in_specs=[pl.BlockSpec((seq, LANES), lambda b: (b, 0)),
                  pl.BlockSpec((1, LANES), lambda b: (0, 0))],
        out_specs=pl.BlockSpec((seq, LANES), lambda b: (b, 0)),
        out_shape=jax.ShapeDtypeStruct((batch * seq, LANES), F32),
        compiler_params=_params(1),
        name="forget_cumsum",
    )(f_logit, b_f)


def _rms_rows(xf, g):
    return xf * lax.rsqrt(jnp.mean(xf * xf, axis=-1, keepdims=True) + RMS_EPS) * g


def _head_cols(a):
    return slice(a * HEAD_DIM, (a + 1) * HEAD_DIM)


def _store_values_transposed(v_ref, vt_ref):
    for a in range(ATT_HEADS):
        for s0 in range(0, v_ref.shape[0], ATT_KEYS):
            vt_ref[a, :, s0:s0 + ATT_KEYS] = v_ref[s0:s0 + ATT_KEYS, _head_cols(a)].astype(F32).T.astype(BF16)


def _fox_kernel(q_ref, k_ref, v_ref, gate_ref, ckey_ref, cqry_ref, gq_ref, gk_ref, gm_ref,
                o_ref, kn_ref, vt_ref, m_ref, l_ref, acc_ref):
    qi = pl.program_id(2)
    seq = k_ref.shape[0]
    nq = q_ref.shape[0]
    blk = ATT_KEYS
    per = nq // blk
    heads = range(ATT_HEADS)

    @pl.when(qi == 0)
    def _():
        _store_values_transposed(v_ref, vt_ref)
        for a in heads:
            for s0 in range(0, seq, blk):
                kn = _rms_rows(k_ref[s0:s0 + blk, _head_cols(a)].astype(F32), gk_ref[...])
                kn_ref[a, s0:s0 + blk, :] = kn.astype(BF16)

    qt = [(_rms_rows(q_ref[:, _head_cols(a)].astype(F32), gq_ref[...]) * ATTN_SCALE).T.astype(BF16)
          for a in heads]
    c_q = [cqry_ref[a] for a in heads]

    def step(a, j, col0, diagonal):
        cols = slice(col0, nq)
        m_prev = m_ref[a, :, cols]
        ks = pl.ds(pl.multiple_of(j * blk, blk), blk)
        s = jnp.dot(kn_ref[a, ks, :], qt[a][:, cols], preferred_element_type=F32)
        s = s + (c_q[a][:, cols] - ckey_ref[a, ks, :])
        if diagonal:
            key = lax.broadcasted_iota(jnp.int32, s.shape, 0)
            qry = lax.broadcasted_iota(jnp.int32, s.shape, 1)
            s = jnp.where(key <= qry, s, -jnp.inf)
        m_new = jnp.maximum(m_prev, jnp.max(s, axis=0, keepdims=True))
        p = jnp.exp(s - m_new)
        scale = jnp.exp(m_prev - m_new)
        m_ref[a, :, cols] = m_new
        l_ref[a, :, cols] = scale * l_ref[a, :, cols] + jnp.sum(p, axis=0, keepdims=True)
        acc_ref[a, :, cols] = scale * acc_ref[a, :, cols] + jnp.dot(
            vt_ref[a, :, ks], p.astype(BF16), preferred_element_type=F32)

    m_ref[...] = jnp.full(m_ref.shape, -jnp.inf, F32)
    l_ref[...] = jnp.zeros_like(l_ref)
    acc_ref[...] = jnp.zeros_like(acc_ref)

    def full_blocks(it, carry):
        for r in range(per):
            for a in heads:
                step(a, per * it + r, 0, False)
        return carry

    lax.fori_loop(0, qi, full_blocks, 0)
    for r in range(per):
        for a in heads:
            step(a, per * qi + r, r * blk, True)
    for a in heads:
        o = _rms_rows((acc_ref[a] / l_ref[a]).T, gm_ref[a]) * jax.nn.sigmoid(gate_ref[:, _head_cols(a)].astype(F32))
        o_ref[:, _head_cols(a)] = o.astype(o_ref.dtype)


def fox_attention_call(p_fox, c_col, c_row, gq, gk, gm, batch, seq):
    nq = seq // ATT_ROWS
    width = ATT_HEADS * HEAD_DIM
    groups = N_FOX_HEADS // ATT_HEADS
    return pl.pallas_call(
        _fox_kernel,
        grid=(batch, groups, nq),
        in_specs=[
            pl.BlockSpec((ATT_ROWS, width), lambda b, hp, qi: (b * nq + qi, hp)),
            pl.BlockSpec((seq, width), lambda b, hp, qi: (b, groups + hp)),
            pl.BlockSpec((seq, width), lambda b, hp, qi: (b, 2 * groups + hp)),
            pl.BlockSpec((ATT_ROWS, width), lambda b, hp, qi: (b * nq + qi, 3 * groups + hp)),
            pl.BlockSpec((None, ATT_HEADS, seq, 1), lambda b, hp, qi: (b, hp, 0, 0)),
            pl.BlockSpec((None, ATT_HEADS, 1, ATT_ROWS), lambda b, hp, qi: (b, hp, 0, qi)),
            pl.BlockSpec((1, HEAD_DIM), lambda b, hp, qi: (0, 0)),
            pl.BlockSpec((1, HEAD_DIM), lambda b, hp, qi: (0, 0)),
            pl.BlockSpec((ATT_HEADS, 1, HEAD_DIM), lambda b, hp, qi: (hp, 0, 0)),
        ],
        out_specs=pl.BlockSpec((ATT_ROWS, width), lambda b, hp, qi: (b * nq + qi, hp)),
        out_shape=jax.ShapeDtypeStruct((batch * seq, FOX_W), BF16),
        scratch_shapes=[pltpu.VMEM((ATT_HEADS, seq, HEAD_DIM), BF16), pltpu.VMEM((ATT_HEADS, HEAD_DIM, seq), BF16),
                        pltpu.VMEM((ATT_HEADS, 1, ATT_ROWS), F32), pltpu.VMEM((ATT_HEADS, 1, ATT_ROWS), F32),
                        pltpu.VMEM((ATT_HEADS, HEAD_DIM, ATT_ROWS), F32)],
        compiler_params=_params(3),
        name="fox_attention",
    )(p_fox, p_fox, p_fox, p_fox, c_col, c_row, gq, gk, gm)


def _sb_kernel(q_ref, k_ref, v_ref, gm_ref, o_ref, vt_ref, after_ref, later_ref, acc_ref):
    qi = pl.program_id(2)
    nq = q_ref.shape[0]
    blk = ATT_KEYS
    per = nq // blk
    heads = range(ATT_HEADS)

    @pl.when(qi == 0)
    def _():
        _store_values_transposed(v_ref, vt_ref)
        row = lax.broadcasted_iota(jnp.int32, (blk, blk), 0)
        col = lax.broadcasted_iota(jnp.int32, (blk, blk), 1)
        after_ref[...] = (col > row).astype(BF16)

    qt = [(q_ref[:, _head_cols(a)].astype(F32) * ATTN_SCALE).T.astype(BF16) for a in heads]

    def step(a, j, col0, diagonal):
        cols = slice(col0, nq)
        ks = pl.ds(pl.multiple_of(j * blk, blk), blk)
        z = jnp.dot(k_ref[ks, _head_cols(a)], qt[a][:, cols], preferred_element_type=F32)
        softplus = jnp.maximum(z, 0.0) + jnp.log(1.0 + jnp.exp(-jnp.abs(z)))
        log_1m = -softplus
        if diagonal:
            key = lax.broadcasted_iota(jnp.int32, z.shape, 0)
            qry = lax.broadcasted_iota(jnp.int32, z.shape, 1)
            strict = key < qry
            log_1m = jnp.where(strict, log_1m, 0.0)
        hi = log_1m.astype(BF16)
        lo = (log_1m - hi.astype(F32)).astype(BF16)
        later = later_ref[a, :, cols]
        tail = (jnp.dot(after_ref[...], hi, preferred_element_type=F32)
                + jnp.dot(after_ref[...], lo, preferred_element_type=F32)) + later
        w = jnp.exp((z - softplus) + tail)
        if diagonal:
            w = jnp.where(strict, w, 0.0)
        acc_ref[a, :, cols] += jnp.dot(vt_ref[a, :, ks], w.astype(BF16), preferred_element_type=F32)
        later_ref[a, :, cols] = later + jnp.sum(log_1m, axis=0, keepdims=True)

    later_ref[...] = jnp.zeros_like(later_ref)
    acc_ref[...] = jnp.zeros_like(acc_ref)
    for r in reversed(range(per)):
        for a in heads:
            step(a, per * qi + r, r * blk, True)

    def full_blocks(it, carry):
        for r in range(per):
            for a in heads:
                step(a, per * (qi - it) - 1 - r, 0, False)
        return carry

    lax.fori_loop(0, qi, full_blocks, 0)
    for a in heads:
        o_ref[:, _head_cols(a)] = _rms_rows(acc_ref[a].T, gm_ref[a]).astype(o_ref.dtype)


def sb_attention_call(p_sb, gm, batch, seq):
    nq = seq // ATT_ROWS
    width = ATT_HEADS * HEAD_DIM
    groups = N_SB_HEADS // ATT_HEADS
    return pl.pallas_call(
        _sb_kernel,
        grid=(batch, groups, nq),
        in_specs=[
            pl.BlockSpec((ATT_ROWS, width), lambda b, hp, qi: (b * nq + qi, hp)),
            pl.BlockSpec((seq, width), lambda b, hp, qi: (b, groups + hp)),
            pl.BlockSpec((seq, width), lambda b, hp, qi: (b, 2 * groups + hp)),
            pl.BlockSpec((ATT_HEADS, 1, HEAD_DIM), lambda b, hp, qi: (N_FOX_HEADS // ATT_HEADS + hp, 0, 0)),
        ],
        out_specs=pl.BlockSpec((ATT_ROWS, width), lambda b, hp, qi: (b * nq + qi, hp)),
        out_shape=jax.ShapeDtypeStruct((batch * seq, SB_W), BF16),
        scratch_shapes=[pltpu.VMEM((ATT_HEADS, HEAD_DIM, seq), BF16), pltpu.VMEM((ATT_KEYS, ATT_KEYS), BF16),
                        pltpu.VMEM((ATT_HEADS, 1, ATT_ROWS), F32), pltpu.VMEM((ATT_HEADS, HEAD_DIM, ATT_ROWS), F32)],
        compiler_params=_params(3),
        name="sb_attention",
    )(p_sb, p_sb, p_sb, gm)


def _xattn_kernel(q_ref, kv_ref, o_ref):
    for hd in range(XA_HEADS):
        cols = _head_cols(hd)
        q = q_ref[:, cols]
        k = kv_ref[:, cols]
        v = kv_ref[:, XA_W + hd * HEAD_DIM:XA_W + (hd + 1) * HEAD_DIM]
        s = lax.dot_general(q, k, _NT_DIMS, preferred_element_type=F32) * ATTN_SCALE
        p = jnp.exp(s - jnp.max(s, axis=-1, keepdims=True))
        p = p / jnp.sum(p, axis=-1, keepdims=True)
        o_ref[:, cols] = jnp.dot(p.astype(BF16), v, preferred_element_type=F32).astype(o_ref.dtype)


def cross_attention_call(q, kv, seq, n_mem):
    rows = q.shape[0]
    per_batch = seq // XA_ROWS
    return pl.pallas_call(
        _xattn_kernel,
        grid=(rows // XA_ROWS,),
        in_specs=[pl.BlockSpec((XA_ROWS, XA_W), lambda i: (i, 0)),
                  pl.BlockSpec((n_mem, 2 * XA_W), lambda i: (i // per_batch, 0))],
        out_specs=pl.BlockSpec((XA_ROWS, XA_W), lambda i: (i, 0)),
        out_shape=jax.ShapeDtypeStruct((rows, XA_W), BF16),
        compiler_params=_params(1),
        name="cross_attention",
    )(q, kv)


def _router_kernel(h_ref, w_ref, b_ref, idx_ref, gate_ref, rank_ref, count_ref, run_ref):
    step = pl.program_id(0)
    rows = h_ref.shape[0]

    @pl.when(step == 0)
    def _():
        run_ref[...] = jnp.zeros_like(run_ref)

    logits = jnp.dot(h_ref[...], w_ref[...], preferred_element_type=F32,
                     precision=lax.Precision.HIGHEST) + b_ref[...]
    lane = lax.broadcasted_iota(jnp.int32, (rows, LANES), 1).astype(F32)
    work = jnp.where(lane < N_EXPERTS, logits, -jnp.inf)
    picks, values = [], []
    for _ in range(TOP_K):
        best = jnp.max(work, axis=-1, keepdims=True)
        first = jnp.min(jnp.where(work == best, lane, float(LANES)), axis=-1, keepdims=True)
        hit = lane == first
        picks.append((first, hit))
        values.append(best)
        work = jnp.where(hit, -jnp.inf, work)
    expv = [jnp.exp(v - values[0]) for v in values]
    denom = sum(expv)

    chosen = sum(hit.astype(F32) for _, hit in picks)
    r = lax.broadcasted_iota(jnp.int32, (rows, rows), 0)
    c = lax.broadcasted_iota(jnp.int32, (rows, rows), 1)
    before = (c < r).astype(BF16)
    prefix = jnp.dot(before, chosen.astype(BF16), preferred_element_type=F32) + run_ref[...]

    idx_out = jnp.zeros((rows, LANES), jnp.int32)
    gate_out = jnp.zeros((rows, LANES), F32)
    rank_out = jnp.zeros((rows, LANES), jnp.int32)
    for k, (first, hit) in enumerate(picks):
        rank_k = jnp.sum(jnp.where(hit, prefix, 0.0), axis=-1, keepdims=True).astype(jnp.int32)
        idx_out = jnp.where(lane == k, first.astype(jnp.int32), idx_out)
        gate_out = jnp.where(lane == k, expv[k] / denom, gate_out)
        rank_out = jnp.where(lane == k, rank_k, rank_out)
    idx_ref[...] = idx_out
    gate_ref[...] = gate_out
    rank_ref[...] = rank_out
    run_ref[...] += jnp.sum(chosen, axis=0, keepdims=True)
    count_ref[...] = run_ref[...]


def router_call(h, w_pad, b_pad):
    rows, d = h.shape
    tok = pl.BlockSpec((RT_ROWS, LANES), lambda i: (i, 0))
    return pl.pallas_call(
        _router_kernel,
        grid=(rows // RT_ROWS,),
        in_specs=[pl.BlockSpec((RT_ROWS, d), lambda i: (i, 0)),
                  pl.BlockSpec((d, LANES), lambda i: (0, 0)),
                  pl.BlockSpec((1, LANES), lambda i: (0, 0))],
        out_specs=[tok, tok, tok, pl.BlockSpec((1, LANES), lambda i: (0, 0))],
        out_shape=[jax.ShapeDtypeStruct((rows, LANES), jnp.int32),
                   jax.ShapeDtypeStruct((rows, LANES), F32),
                   jax.ShapeDtypeStruct((rows, LANES), jnp.int32),
                   jax.ShapeDtypeStruct((1, LANES), F32)],
        scratch_shapes=[pltpu.VMEM((1, LANES), F32)],
        compiler_params=_params(1),
        name="router",
    )(h, w_pad, b_pad)


def _slab_copy(zeros_ref, dst_hbm, sem, row):
    return pltpu.make_async_copy(zeros_ref, dst_hbm.at[pl.ds(row, MOE_ROWS)], sem)


def _token_copy(src_ref, dst_ref, sem, src_row, dst_row):
    return pltpu.make_async_copy(src_ref.at[pl.ds(src_row, 1)], dst_ref.at[pl.ds(dst_row, 1)], sem)


def _dispatch_kernel(fill_ref, dest_ref, h_ref, xs_hbm, zeros_ref, slab_sem, row_sem):
    tokens = h_ref.shape[0]

    @pl.when(pl.program_id(0) == 0)
    def _():
        zeros_ref[...] = jnp.zeros_like(zeros_ref)

        def group_slab(e, carry):
            copy = _slab_copy(zeros_ref, xs_hbm, slab_sem, fill_ref[e])
            copy.start()
            copy.wait()
            return carry

        lax.fori_loop(0, N_EXPERTS, group_slab, 0)
        tail0, n_tail = fill_ref[N_EXPERTS], fill_ref[N_EXPERTS + 1]

        def tail_start(s, carry):
            _slab_copy(zeros_ref, xs_hbm, slab_sem, tail0 + s * MOE_ROWS).start()
            return carry

        def tail_wait(s, carry):
            _slab_copy(zeros_ref, xs_hbm, slab_sem, tail0).wait()
            return carry

        lax.fori_loop(0, n_tail, tail_start, 0)
        lax.fori_loop(0, n_tail, tail_wait, 0)

    def start(t, carry):
        for k in range(TOP_K):
            _token_copy(h_ref, xs_hbm, row_sem, t, dest_ref[0, t * TOP_K + k]).start(priority=k % 2)
        return carry

    def wait(t, carry):
        for k in range(TOP_K):
            _token_copy(h_ref, xs_hbm, row_sem, t, 0).wait()
        return carry

    lax.fori_loop(0, tokens, start, 0, unroll=DMA_UNROLL)
    lax.fori_loop(0, tokens, wait, 0, unroll=DMA_UNROLL)


def dispatch_call(fill, dest, slabs, p_rows):
    tokens = slabs.shape[0]
    steps = tokens // DISPATCH_ROWS
    slab = slabs.shape[1:]
    grid_spec = pltpu.PrefetchScalarGridSpec(
        num_scalar_prefetch=1,
        grid=(steps,),
        in_specs=[pl.BlockSpec((None, 1, DISPATCH_ROWS * TOP_K), lambda i, f: (i, 0, 0), memory_space=pltpu.SMEM),
                  pl.BlockSpec((DISPATCH_ROWS,) + slab, lambda i, f: (i, 0, 0))],
        out_specs=pl.BlockSpec(memory_space=pl.ANY),
        scratch_shapes=[pltpu.VMEM((MOE_ROWS,) + slab, slabs.dtype),
                        pltpu.SemaphoreType.DMA(()), pltpu.SemaphoreType.DMA(())],
    )
    return pl.pallas_call(
        _dispatch_kernel,
        grid_spec=grid_spec,
        out_shape=jax.ShapeDtypeStruct((p_rows + MOE_ROWS,) + slab, slabs.dtype),
        compiler_params=_params(1),
        name="moe_dispatch",
    )(fill, dest.reshape(steps, 1, DISPATCH_ROWS * TOP_K), slabs)


def _tile(i, nv_ref):
    return jnp.maximum(jnp.minimum(i, nv_ref[0] - 1), 0)


def _expert_changed(i, te_ref):
    return jnp.logical_or(i == 0, te_ref[i] != te_ref[jnp.maximum(i - 1, 0)])


def _for_real_rows(i, nv_ref, tr_ref, out_ref, compute):
    used = i < nv_ref[0]
    whole = jnp.logical_and(used, tr_ref[i] > MOE_HALF)

    @pl.when(whole)
    def _():
        out_ref[...] = compute(slice(None))

    @pl.when(jnp.logical_and(used, jnp.logical_not(whole)))
    def _():
        out_ref[:MOE_HALF] = jnp.zeros_like(out_ref[:MOE_HALF])
        out_ref[MOE_HALF:] = compute(slice(MOE_HALF, MOE_ROWS))

    @pl.when(jnp.logical_not(used))
    def _():
        out_ref[...] = jnp.zeros_like(out_ref)


def _moe_up_kernel(te_ref, nv_ref, tr_ref, x_ref, wg_ref, wu_ref, bg_ref, bu_ref, a_ref, wgb_ref, wub_ref):
    i = pl.program_id(1)

    @pl.when(_expert_changed(i, te_ref))
    def _():
        wgb_ref[...] = wg_ref[...].astype(BF16)
        wub_ref[...] = wu_ref[...].astype(BF16)

    def compute(rows):
        x = x_ref[rows].reshape(-1, x_ref.shape[1] * x_ref.shape[2])
        g = jnp.dot(x, wgb_ref[...], preferred_element_type=F32) + bg_ref[...]
        u = jnp.dot(x, wub_ref[...], preferred_element_type=F32) + bu_ref[...]
        g = jnp.minimum(g, SWIGLU_LIMIT)
        u = jnp.clip(u, -SWIGLU_LIMIT, SWIGLU_LIMIT)
        return ((u + 1.0) * g * jax.nn.sigmoid(SWIGLU_ALPHA * g)).astype(a_ref.dtype)

    _for_real_rows(i, nv_ref, tr_ref, a_ref, compute)


def moe_up_call(tile_expert, n_valid, tile_rows, xs, w_up, b_up):
    d = xs.shape[1] * xs.shape[2]
    n_tiles = tile_expert.shape[0]
    p_rows = n_tiles * MOE_ROWS
    n_col = D_EXPERT // UP_COLS
    grid_spec = pltpu.PrefetchScalarGridSpec(
        num_scalar_prefetch=3,
        grid=(n_col, n_tiles),
        in_specs=[
            pl.BlockSpec((MOE_ROWS,) + xs.shape[1:], lambda n, i, te, nv, tr: (_tile(i, nv), 0, 0)),
            pl.BlockSpec((None, d, UP_COLS), lambda n, i, te, nv, tr: (te[_tile(i, nv)], 0, n)),
            pl.BlockSpec((None, d, UP_COLS), lambda n, i, te, nv, tr: (te[_tile(i, nv)], 0, n_col + n)),
            pl.BlockSpec((None, 1, UP_COLS), lambda n, i, te, nv, tr: (te[_tile(i, nv)], 0, n)),
            pl.BlockSpec((None, 1, UP_COLS), lambda n, i, te, nv, tr: (te[_tile(i, nv)], 0, n_col + n)),
        ],
        out_specs=pl.BlockSpec((MOE_ROWS, UP_COLS), lambda n, i, te, nv, tr: (i, n)),
        scratch_shapes=[pltpu.VMEM((d, UP_COLS), BF16), pltpu.VMEM((d, UP_COLS), BF16)],
    )
    return pl.pallas_call(
        _moe_up_kernel,
        grid_spec=grid_spec,
        out_shape=jax.ShapeDtypeStruct((p_rows, D_EXPERT), BF16),
        compiler_params=_params(2),
        name="moe_up",
    )(tile_expert, n_valid, tile_rows, xs, w_up, w_up, b_up, b_up)


def _moe_down_kernel(te_ref, nv_ref, tr_ref, a_ref, w_ref, b_ref, y_ref, wb_ref):
    i = pl.program_id(1)

    @pl.when(_expert_changed(i, te_ref))
    def _():
        wb_ref[...] = w_ref[...].astype(BF16)

    def compute(rows):
        y = jnp.dot(a_ref[rows, :], wb_ref[...], preferred_element_type=F32) + b_ref[...]
        return y.reshape((y.shape[0],) + y_ref.shape[1:]).astype(y_ref.dtype)

    _for_real_rows(i, nv_ref, tr_ref, y_ref, compute)


def moe_down_call(tile_expert, n_valid, tile_rows, act, w_down, b_down):
    p_rows = act.shape[0]
    n_tiles = p_rows // MOE_ROWS
    d_out = w_down.shape[2]
    grid_spec = pltpu.PrefetchScalarGridSpec(
        num_scalar_prefetch=3,
        grid=(d_out // DOWN_COLS, n_tiles),
        in_specs=[
            pl.BlockSpec((MOE_ROWS, D_EXPERT), lambda n, i, te, nv, tr: (_tile(i, nv), 0)),
            pl.BlockSpec((None, D_EXPERT, DOWN_COLS), lambda n, i, te, nv, tr: (te[_tile(i, nv)], 0, n)),
            pl.BlockSpec((None, 1, DOWN_COLS), lambda n, i, te, nv, tr: (te[_tile(i, nv)], 0, n)),
        ],
        out_specs=pl.BlockSpec((MOE_ROWS, DOWN_COLS // LANES, LANES), lambda n, i, te, nv, tr: (i, n, 0)),
        scratch_shapes=[pltpu.VMEM((D_EXPERT, DOWN_COLS), BF16)],
    )
    return pl.pallas_call(
        _moe_down_kernel,
        grid_spec=grid_spec,
        out_shape=jax.ShapeDtypeStruct((p_rows, d_out // LANES, LANES), BF16),
        compiler_params=_params(2),
        name="moe_down",
    )(tile_expert, n_valid, tile_rows, act, w_down, b_down)


def _combine_kernel(dest_ref, next_ref, y_hbm, gate_ref, h_ref, g_ref, b_ref, o_ref, buf_ref, sems):
    step = pl.program_id(0)
    rows = h_ref.shape[0]
    slot = step % 2

    def issue(idx_ref, to_slot):
        def start(r, carry):
            for k in range(TOP_K):
                _token_copy(y_hbm, buf_ref.at[to_slot, k], sems.at[to_slot], idx_ref[0, r * TOP_K + k], r
                            ).start(priority=k % 2)
            return carry
        lax.fori_loop(0, rows, start, 0, unroll=DMA_UNROLL)

    @pl.when(step == 0)
    def _():
        issue(dest_ref, 0)

    @pl.when(step + 1 < pl.num_programs(0))
    def _():
        issue(next_ref, 1 - slot)

    def wait(r, carry):
        for k in range(TOP_K):
            _token_copy(y_hbm, buf_ref.at[slot, k], sems.at[slot], 0, r).wait()
        return carry

    lax.fori_loop(0, rows, wait, 0, unroll=DMA_UNROLL)
    ff = None
    for k in range(TOP_K):
        term = gate_ref[:, k:k + 1] * buf_ref[slot, k].astype(F32).reshape(rows, D_MODEL)
        ff = term if ff is None else ff + term
    o_ref[...] = _layer_norm_rows(DEEPNORM_ALPHA * h_ref[...] + ff, g_ref[...], b_ref[...])


def combine_call(dest, y, gate, h, g, b):
    rows, d = h.shape
    steps = rows // COMBINE_ROWS
    dest3 = dest.reshape(steps, 1, COMBINE_ROWS * TOP_K)
    idx_block = (None, 1, COMBINE_ROWS * TOP_K)
    return pl.pallas_call(
        _combine_kernel,
        grid=(steps,),
        in_specs=[pl.BlockSpec(idx_block, lambda i: (i, 0, 0), memory_space=pltpu.SMEM),
                  pl.BlockSpec(idx_block, lambda i: (jnp.minimum(i + 1, steps - 1), 0, 0), memory_space=pltpu.SMEM),
                  pl.BlockSpec(memory_space=pl.ANY),
                  pl.BlockSpec((COMBINE_ROWS, LANES), lambda i: (i, 0)),
                  pl.BlockSpec((COMBINE_ROWS, d), lambda i: (i, 0)),
                  pl.BlockSpec((1, d), lambda i: (0, 0)),
                  pl.BlockSpec((1, d), lambda i: (0, 0))],
        out_specs=pl.BlockSpec((COMBINE_ROWS, d), lambda i: (i, 0)),
        out_shape=jax.ShapeDtypeStruct((rows, d), F32),
        scratch_shapes=[pltpu.VMEM((2, TOP_K, COMBINE_ROWS, ROW_TILES, LANES), y.dtype),
                        pltpu.SemaphoreType.DMA((2,))],
        compiler_params=_params(1),
        name="moe_combine",
    )(dest3, dest3, y, gate, h, g.reshape(1, d), b.reshape(1, d))


def _pad_lanes(a, rows):
    out = jnp.zeros((rows, LANES), a.dtype)
    return out.at[:, :a.shape[-1]].set(a.reshape(rows, -1))


def kernel(x, mem, ln_in_g, ln_in_b, w_in, b_f, fox_q_norm_g, fox_k_norm_g, mix_norm_g, w_out, ln_mix_g, ln_mix_b, mem_ln_g, mem_ln_b, xa_wq, xa_wkv, xa_wo, ln_xa_g, ln_xa_b, router_w, router_b, w_up, b_up, w_down, b_down, ln_moe_g, ln_moe_b):
    batch, seq, d = x.shape
    n_mem = mem.shape[1]
    tokens = batch * seq
    assert w_in.shape[0] == DEPTH == 1 and d == D_MODEL

    h0, h0b = layer_norm_call(x.reshape(tokens, d), ln_in_g, ln_in_b, ("f32", "bf16"))
    w_in_t = jnp.swapaxes(w_in, 1, 2)[0]
    sb0 = 4 * FOX_W + N_FOX_HEADS
    p_fox = matmul_call([(h0b, w_in_t, 0, 0, True)], 4 * FOX_W, BF16, name="in_proj_fox")
    f_logit = matmul_call([(h0b, w_in_t, 0, 4 * FOX_W, True)], LANES, F32, tn=LANES, name="in_proj_forget")
    p_sb = matmul_call([(h0b, w_in_t, 0, sb0, True)], 3 * SB_W, BF16, name="in_proj_sb")

    c = forget_cumsum_call(f_logit, _pad_lanes(b_f[0], 1), batch, seq)
    c = c.reshape(batch, seq, LANES)[:, :, :N_FOX_HEADS].transpose(0, 2, 1)
    gm = mix_norm_g[0].reshape(-1, 1, HEAD_DIM)
    o_f = fox_attention_call(p_fox, c[..., None], c[:, :, None, :],
                             fox_q_norm_g[0].reshape(1, HEAD_DIM), fox_k_norm_g[0].reshape(1, HEAD_DIM),
                             gm, batch, seq)
    o_s = sb_attention_call(p_sb, gm, batch, seq)
    r1 = matmul_call([(o_f, w_out[0], 0, 0, False), (o_s, w_out[0], FOX_W, 0, False)], d, F32,
                     res=h0, alpha=DEEPNORM_ALPHA, name="out_proj")
    h1, h1b = layer_norm_call(r1, ln_mix_g[0], ln_mix_b[0], ("f32", "bf16"))

    (mem_nb,) = layer_norm_call(mem.reshape(batch * n_mem, d), mem_ln_g[0], mem_ln_b[0], ("bf16",))
    kv = matmul_call([(mem_nb, xa_wkv[0], 0, 0, False)], 2 * XA_W, BF16, name="xa_kv_proj")
    qx = matmul_call([(h1b, xa_wq[0], 0, 0, False)], XA_W, BF16, name="xa_q_proj")
    ox = cross_attention_call(qx, kv, seq, n_mem)
    r2 = matmul_call([(ox, xa_wo[0], 0, 0, False)], d, F32, res=h1, alpha=DEEPNORM_ALPHA, name="xa_out_proj")
    h2, h2s = layer_norm_call(r2, ln_xa_g[0], ln_xa_b[0], ("f32", "slab"))

    idx, gate, rank, counts = router_call(h2, _pad_lanes(router_w[0], d), _pad_lanes(router_b[0], 1))
    counts = counts[0, :N_EXPERTS].astype(jnp.int32)
    padded = (counts + MOE_ROWS - 1) // MOE_ROWS * MOE_ROWS
    ends = jnp.cumsum(padded)
    starts = ends - padded
    first = ends - counts
    dest = (first[idx[:, :TOP_K]] + rank[:, :TOP_K]).reshape(-1)
    p_rows = tokens * TOP_K + N_EXPERTS * MOE_ROWS
    n_tiles = p_rows // MOE_ROWS
    tile_row0 = jnp.arange(n_tiles, dtype=jnp.int32) * MOE_ROWS
    tile_expert = jnp.minimum(jnp.sum(ends[None, :] <= tile_row0[:, None], axis=1), N_EXPERTS - 1).astype(jnp.int32)
    tile_rows = jnp.clip(tile_row0 + MOE_ROWS - first[tile_expert], 0, MOE_ROWS).astype(jnp.int32)
    n_valid = (ends[-1:] // MOE_ROWS).astype(jnp.int32)
    fill = jnp.concatenate([starts, ends[-1:], (p_rows + MOE_ROWS - ends[-1:]) // MOE_ROWS]).astype(jnp.int32)

    xs = dispatch_call(fill, dest, h2s, p_rows)
    act = moe_up_call(tile_expert, n_valid, tile_rows, xs, w_up[0], b_up[0].reshape(N_EXPERTS, 1, -1))
    y = moe_down_call(tile_expert, n_valid, tile_rows, act, w_down[0], b_down[0].reshape(N_EXPERTS, 1, -1))
    out = combine_call(dest, y, gate, h2, ln_moe_g[0], ln_moe_b[0])
    return out.reshape(batch, seq, d)
```

```python
import functools

import jax
import jax.numpy as jnp
from jax import lax
from jax.experimental import pallas as pl
from jax.experimental.pallas import tpu as pltpu

F32 = jnp.float32
BF16 = jnp.bfloat16

D_MODEL = 4096
HEAD_DIM = 128
N_FOX_HEADS = 16
N_SB_HEADS = 16
FOX_W = N_FOX_HEADS * HEAD_DIM
SB_W = N_SB_HEADS * HEAD_DIM
XA_HEADS = 4
XA_W = XA_HEADS * HEAD_DIM
N_EXPERTS = 32
TOP_K = 4
D_EXPERT = D_MODEL // 4
SWIGLU_LIMIT = 7.0
SWIGLU_ALPHA = 1.702
LN_EPS = 1e-5
RMS_EPS = 1e-6
DEPTH = 1
DEEPNORM_ALPHA = (2 * DEPTH) ** 0.25
ATTN_SCALE = HEAD_DIM ** -0.5

LANES = 128
BF16_SUBLANES = 16
V7X_VMEM_BYTES = 64 * 1024 * 1024
VMEM_LIMIT = V7X_VMEM_BYTES - 8 * 1024 * 1024

LN_ROWS = 256
MM_ROWS = 2048
MM_SUB = 1024
MM_COLS = 256
ATT_ROWS = 2048
ATT_KEYS = 256
ATT_HEADS = 2
XA_ROWS = 512
RT_ROWS = 256
MOE_ROWS = 512
UP_COLS = 512
DOWN_COLS = 2048
DISPATCH_ROWS = 512
COMBINE_ROWS = 128
DMA_UNROLL = 8

ROW_TILES = D_MODEL // LANES


def _params(n_axes, vmem=VMEM_LIMIT):
    return pltpu.CompilerParams(dimension_semantics=("arbitrary",) * n_axes,
                                vmem_limit_bytes=vmem)


def _layer_norm_rows(xf, g, b):
    mu = jnp.mean(xf, axis=-1, keepdims=True)
    xc = xf - mu
    var = jnp.mean(xc * xc, axis=-1, keepdims=True)
    return xc * lax.rsqrt(var + LN_EPS) * g + b


def _ln_kernel(x_ref, g_ref, b_ref, *out_refs, kinds):
    y = _layer_norm_rows(x_ref[...].astype(F32), g_ref[...], b_ref[...])
    for ref, kind in zip(out_refs, kinds):
        if kind == "f32":
            ref[...] = y
        elif kind == "bf16":
            ref[...] = y.astype(BF16)
        else:
            ref[...] = y.reshape(ref.shape).astype(BF16)


def layer_norm_call(x, g, b, kinds):
    rows, d = x.shape
    out_shape, out_specs = [], []
    for kind in kinds:
        if kind == "slab":
            out_shape.append(jax.ShapeDtypeStruct((rows, ROW_TILES, LANES), BF16))
            out_specs.append(pl.BlockSpec((LN_ROWS, ROW_TILES, LANES), lambda i: (i, 0, 0)))
        else:
            out_shape.append(jax.ShapeDtypeStruct((rows, d), F32 if kind == "f32" else BF16))
            out_specs.append(pl.BlockSpec((LN_ROWS, d), lambda i: (i, 0)))
    return pl.pallas_call(
        functools.partial(_ln_kernel, kinds=kinds),
        grid=(rows // LN_ROWS,),
        in_specs=[pl.BlockSpec((LN_ROWS, d), lambda i: (i, 0)),
                  pl.BlockSpec((1, d), lambda i: (0, 0)),
                  pl.BlockSpec((1, d), lambda i: (0, 0))],
        out_specs=out_specs,
        out_shape=out_shape,
        compiler_params=_params(1),
        name="layer_norm",
    )(x, g.reshape(1, d), b.reshape(1, d))


_NT_DIMS = (((1,), (1,)), ((), ()))


def _mm_kernel(*refs, transposed, has_res, alpha, sub):
    n_pairs = len(transposed)
    lhs_refs = refs[:n_pairs]
    w_refs = refs[n_pairs:2 * n_pairs]
    pos = 2 * n_pairs
    res_ref = refs[pos] if has_res else None
    pos += int(has_res)
    out_ref = refs[pos]
    wb_refs = refs[pos + 1:]
    for w_ref, wb_ref in zip(w_refs, wb_refs):
        wb_ref[...] = w_ref[...].astype(BF16)
    rows = out_ref.shape[0]

    def body(r, carry):
        sl = pl.ds(pl.multiple_of(r * sub, sub), sub)
        acc = None
        for lhs_ref, wb_ref, tr in zip(lhs_refs, wb_refs, transposed):
            if tr:
                part = lax.dot_general(lhs_ref[sl, :], wb_ref[...], _NT_DIMS, preferred_element_type=F32)
            else:
                part = jnp.dot(lhs_ref[sl, :], wb_ref[...], preferred_element_type=F32)
            acc = part if acc is None else acc + part
        if has_res:
            acc = acc + alpha * res_ref[sl, :]
        out_ref[sl, :] = acc.astype(out_ref.dtype)
        return carry

    lax.fori_loop(0, rows // sub, body, 0)


def matmul_call(pairs, n_cols, out_dtype, res=None, alpha=1.0, tn=MM_COLS, tm=MM_ROWS, name="matmul"):
    m = pairs[0][0].shape[0]
    tm = min(tm, m)
    sub = min(MM_SUB, tm)
    in_specs, args, scratch = [], [], []
    for lhs, _, _, _, _ in pairs:
        in_specs.append(pl.BlockSpec((tm, lhs.shape[1]), lambda i, j: (i, 0)))
        args.append(lhs)
    for lhs, w, k0, col0, tr in pairs:
        k = lhs.shape[1]
        if tr:
            in_specs.append(pl.BlockSpec(
                (pl.Element(tn), pl.Element(k)),
                lambda i, j, k0=k0, col0=col0: (pl.multiple_of(col0 + j * tn, BF16_SUBLANES), k0)))
            scratch.append(pltpu.VMEM((tn, k), BF16))
        else:
            in_specs.append(pl.BlockSpec((k, tn), lambda i, j, rb=k0 // k, cb=col0 // tn: (rb, j + cb)))
            scratch.append(pltpu.VMEM((k, tn), BF16))
        args.append(w)
    if res is not None:
        in_specs.append(pl.BlockSpec((tm, tn), lambda i, j: (i, j)))
        args.append(res)
    return pl.pallas_call(
        functools.partial(_mm_kernel, transposed=tuple(p[4] for p in pairs), has_res=res is not None,
                          alpha=alpha, sub=sub),
        grid=(m // tm, n_cols // tn),
        in_specs=in_specs,
        out_specs=pl.BlockSpec((tm, tn), lambda i, j: (i, j)),
        out_shape=jax.ShapeDtypeStruct((m, n_cols), out_dtype),
        scratch_shapes=scratch,
        compiler_params=_params(2),
        name=name,
    )(*args)


def _log_sigmoid(x):
    return jnp.minimum(x, 0.0) - jnp.log1p(jnp.exp(-jnp.abs(x)))


def _split3_bf16(x):
    a = x.astype(BF16)
    r = x - a.astype(F32)
    b = r.astype(BF16)
    c = (r - b.astype(F32)).astype(BF16)
    return a, b, c


def _forget_cumsum_kernel(f_ref, b_ref, c_ref, *, chunk):
    seq = f_ref.shape[0]
    row = lax.broadcasted_iota(jnp.int32, (chunk, chunk), 0)
    col = lax.broadcasted_iota(jnp.int32, (chunk, chunk), 1)
    tri = (col <= row).astype(BF16)
    carry = jnp.zeros((1, f_ref.shape[1]), F32)
    for c0 in range(0, seq, chunk):
        lf = _log_sigmoid(f_ref[c0:c0 + chunk, :] + b_ref[...])
        part = sum(jnp.dot(tri, piece, preferred_element_type=F32) for piece in _split3_bf16(lf))
        cs = part + carry
        c_ref[c0:c0 + chunk, :] = cs
        carry = cs[chunk - 1:chunk, :]


def forget_cumsum_call(f_logit, b_f, batch, seq):
    return pl.pallas_call(
        functools.partial(_forget_cumsum_kernel, chunk=ATT_KEYS),
        grid=(batch,),
        in_specs=[pl.BlockSpec((seq, LANES), lambda b: (b, 0)),
                  pl.BlockSpec((1, LANES), lambda b: (0, 0))],
        out_specs=pl.BlockSpec((seq, LANES), lambda b: (b, 0)),
        out_shape=jax.ShapeDtypeStruct((batch * seq, LANES), F32),
        compiler_params=_params(1),
        name="forget_cumsum",
    )(f_logit, b_f)


def _rms_rows(xf, g):
    return xf * lax.rsqrt(jnp.mean(xf * xf, axis=-1, keepdims=True) + RMS_EPS) * g


def _head_cols(a):
    return slice(a * HEAD_DIM, (a + 1) * HEAD_DIM)


def _store_values_transposed(v_ref, vt_ref):
    for a in range(ATT_HEADS):
        for s0 in range(0, v_ref.shape[0], ATT_KEYS):
            vt_ref[a, :, s0:s0 + ATT_KEYS] = v_ref[s0:s0 + ATT_KEYS, _head_cols(a)].astype(F32).T.astype(BF16)


def _fox_kernel(q_ref, k_ref, v_ref, gate_ref, ckey_ref, cqry_ref, gq_ref, gk_ref, gm_ref,
                o_ref, kn_ref, vt_ref, m_ref, l_ref, acc_ref):
    qi = pl.program_id(2)
    seq = k_ref.shape[0]
    nq = q_ref.shape[0]
    blk = ATT_KEYS
    per = nq // blk
    heads = range(ATT_HEADS)

    @pl.when(qi == 0)
    def _():
        _store_values_transposed(v_ref, vt_ref)
        for a in heads:
            for s0 in range(0, seq, blk):
                kn = _rms_rows(k_ref[s0:s0 + blk, _head_cols(a)].astype(F32), gk_ref[...])
                kn_ref[a, s0:s0 + blk, :] = kn.astype(BF16)

    qt = [(_rms_rows(q_ref[:, _head_cols(a)].astype(F32), gq_ref[...]) * ATTN_SCALE).T.astype(BF16)
          for a in heads]
    c_q = [cqry_ref[a] for a in heads]

    def step(a, j, col0, diagonal):
        cols = slice(col0, nq)
        m_prev = m_ref[a, :, cols]
        ks = pl.ds(pl.multiple_of(j * blk, blk), blk)
        s = jnp.dot(kn_ref[a, ks, :], qt[a][:, cols], preferred_element_type=F32)
        s = s + (c_q[a][:, cols] - ckey_ref[a, ks, :])
        if diagonal:
            key = lax.broadcasted_iota(jnp.int32, s.shape, 0)
            qry = lax.broadcasted_iota(jnp.int32, s.shape, 1)
            s = jnp.where(key <= qry, s, -jnp.inf)
        m_new = jnp.maximum(m_prev, jnp.max(s, axis=0, keepdims=True))
        p = jnp.exp(s - m_new)
        scale = jnp.exp(m_prev - m_new)
        m_ref[a, :, cols] = m_new
        l_ref[a, :, cols] = scale * l_ref[a, :, cols] + jnp.sum(p, axis=0, keepdims=True)
        acc_ref[a, :, cols] = scale * acc_ref[a, :, cols] + jnp.dot(
            vt_ref[a, :, ks], p.astype(BF16), preferred_element_type=F32)

    m_ref[...] = jnp.full(m_ref.shape, -jnp.inf, F32)
    l_ref[...] = jnp.zeros_like(l_ref)
    acc_ref[...] = jnp.zeros_like(acc_ref)

    def full_blocks(it, carry):
        for r in range(per):
            for a in heads:
                step(a, per * it + r, 0, False)
        return carry

    lax.fori_loop(0, qi, full_blocks, 0)
    for r in range(per):
        for a in heads:
            step(a, per * qi + r, r * blk, True)
    for a in heads:
        o = _rms_rows((acc_ref[a] / l_ref[a]).T, gm_ref[a]) * jax.nn.sigmoid(gate_ref[:, _head_cols(a)].astype(F32))
        o_ref[:, _head_cols(a)] = o.astype(o_ref.dtype)


def fox_attention_call(p_fox, c_col, c_row, gq, gk, gm, batch, seq):
    nq = seq // ATT_ROWS
    width = ATT_HEADS * HEAD_DIM
    groups = N_FOX_HEADS // ATT_HEADS
    return pl.pallas_call(
        _fox_kernel,
        grid=(batch, groups, nq),
        in_specs=[
            pl.BlockSpec((ATT_ROWS, width), lambda b, hp, qi: (b * nq + qi, hp)),
            pl.BlockSpec((seq, width), lambda b, hp, qi: (b, groups + hp)),
            pl.BlockSpec((seq, width), lambda b, hp, qi: (b, 2 * groups + hp)),
            pl.BlockSpec((ATT_ROWS, width), lambda b, hp, qi: (b * nq + qi, 3 * groups + hp)),
            pl.BlockSpec((None, ATT_HEADS, seq, 1), lambda b, hp, qi: (b, hp, 0, 0)),
            pl.BlockSpec((None, ATT_HEADS, 1, ATT_ROWS), lambda b, hp, qi: (b, hp, 0, qi)),
            pl.BlockSpec((1, HEAD_DIM), lambda b, hp, qi: (0, 0)),
            pl.BlockSpec((1, HEAD_DIM), lambda b, hp, qi: (0, 0)),
            pl.BlockSpec((ATT_HEADS, 1, HEAD_DIM), lambda b, hp, qi: (hp, 0, 0)),
        ],
        out_specs=pl.BlockSpec((ATT_ROWS, width), lambda b, hp, qi: (b * nq + qi, hp)),
        out_shape=jax.ShapeDtypeStruct((batch * seq, FOX_W), BF16),
        scratch_shapes=[pltpu.VMEM((ATT_HEADS, seq, HEAD_DIM), BF16), pltpu.VMEM((ATT_HEADS, HEAD_DIM, seq), BF16),
                        pltpu.VMEM((ATT_HEADS, 1, ATT_ROWS), F32), pltpu.VMEM((ATT_HEADS, 1, ATT_ROWS), F32),
                        pltpu.VMEM((ATT_HEADS, HEAD_DIM, ATT_ROWS), F32)],
        compiler_params=_params(3),
        name="fox_attention",
    )(p_fox, p_fox, p_fox, p_fox, c_col, c_row, gq, gk, gm)


def _sb_kernel(q_ref, k_ref, v_ref, gm_ref, o_ref, vt_ref, after_ref, later_ref, acc_ref):
    qi = pl.program_id(2)
    nq = q_ref.shape[0]
    blk = ATT_KEYS
    per = nq // blk
    heads = range(ATT_HEADS)

    @pl.when(qi == 0)
    def _():
        _store_values_transposed(v_ref, vt_ref)
        row = lax.broadcasted_iota(jnp.int32, (blk, blk), 0)
        col = lax.broadcasted_iota(jnp.int32, (blk, blk), 1)
        after_ref[...] = (col > row).astype(BF16)

    qt = [(q_ref[:, _head_cols(a)].astype(F32) * ATTN_SCALE).T.astype(BF16) for a in heads]

    def step(a, j, col0, diagonal):
        cols = slice(col0, nq)
        ks = pl.ds(pl.multiple_of(j * blk, blk), blk)
        z = jnp.dot(k_ref[ks, _head_cols(a)], qt[a][:, cols], preferred_element_type=F32)
        softplus = jnp.maximum(z, 0.0) + jnp.log(1.0 + jnp.exp(-jnp.abs(z)))
        log_1m = -softplus
        if diagonal:
            key = lax.broadcasted_iota(jnp.int32, z.shape, 0)
            qry = lax.broadcasted_iota(jnp.int32, z.shape, 1)
            strict = key < qry
            log_1m = jnp.where(strict, log_1m, 0.0)
        hi = log_1m.astype(BF16)
        lo = (log_1m - hi.astype(F32)).astype(BF16)
        later = later_ref[a, :, cols]
        tail = (jnp.dot(after_ref[...], hi, preferred_element_type=F32)
                + jnp.dot(after_ref[...], lo, preferred_element_type=F32)) + later
        w = jnp.exp((z - softplus) + tail)
        if diagonal:
            w = jnp.where(strict, w, 0.0)
        acc_ref[a, :, cols] += jnp.dot(vt_ref[a, :, ks], w.astype(BF16), preferred_element_type=F32)
        later_ref[a, :, cols] = later + jnp.sum(log_1m, axis=0, keepdims=True)

    later_ref[...] = jnp.zeros_like(later_ref)
    acc_ref[...] = jnp.zeros_like(acc_ref)
    for r in reversed(range(per)):
        for a in heads:
            step(a, per * qi + r, r * blk, True)

    def full_blocks(it, carry):
        for r in range(per):
            for a in heads:
                step(a, per * (qi - it) - 1 - r, 0, False)
        return carry

    lax.fori_loop(0, qi, full_blocks, 0)
    for a in heads:
        o_ref[:, _head_cols(a)] = _rms_rows(acc_ref[a].T, gm_ref[a]).astype(o_ref.dtype)


def sb_attention_call(p_sb, gm, batch, seq):
    nq = seq // ATT_ROWS
    width = ATT_HEADS * HEAD_DIM
    groups = N_SB_HEADS // ATT_HEADS
    return pl.pallas_call(
        _sb_kernel,
        grid=(batch, groups, nq),
        in_specs=[
            pl.BlockSpec((ATT_ROWS, width), lambda b, hp, qi: (b * nq + qi, hp)),
            pl.BlockSpec((seq, width), lambda b, hp, qi: (b, groups + hp)),
            pl.BlockSpec((seq, width), lambda b, hp, qi: (b, 2 * groups + hp)),
            pl.BlockSpec((ATT_HEADS, 1, HEAD_DIM), lambda b, hp, qi: (N_FOX_HEADS // ATT_HEADS + hp, 0, 0)),
        ],
        out_specs=pl.BlockSpec((ATT_ROWS, width), lambda b, hp, qi: (b * nq + qi, hp)),
        out_shape=jax.ShapeDtypeStruct((batch * seq, SB_W), BF16),
        scratch_shapes=[pltpu.VMEM((ATT_HEADS, HEAD_DIM, seq), BF16), pltpu.VMEM((ATT_KEYS, ATT_KEYS), BF16),
                        pltpu.VMEM((ATT_HEADS, 1, ATT_ROWS), F32), pltpu.VMEM((ATT_HEADS, HEAD_DIM, ATT_ROWS), F32)],
        compiler_params=_params(3),
        name="sb_attention",
    )(p_sb, p_sb, p_sb, gm)


def _xattn_kernel(q_ref, kv_ref, o_ref):
    for hd in range(XA_HEADS):
        cols = _head_cols(hd)
        q = q_ref[:, cols]
        k = kv_ref[:, cols]
        v = kv_ref[:, XA_W + hd * HEAD_DIM:XA_W + (hd + 1) * HEAD_DIM]
        s = lax.dot_general(q, k, _NT_DIMS, preferred_element_type=F32) * ATTN_SCALE
        p = jnp.exp(s - jnp.max(s, axis=-1, keepdims=True))
        p = p / jnp.sum(p, axis=-1, keepdims=True)
        o_ref[:, cols] = jnp.dot(p.astype(BF16), v, preferred_element_type=F32).astype(o_ref.dtype)


def cross_attention_call(q, kv, seq, n_mem):
    rows = q.shape[0]
    per_batch = seq // XA_ROWS
    return pl.pallas_call(
        _xattn_kernel,
        grid=(rows // XA_ROWS,),
        in_specs=[pl.BlockSpec((XA_ROWS, XA_W), lambda i: (i, 0)),
                  pl.BlockSpec((n_mem, 2 * XA_W), lambda i: (i // per_batch, 0))],
        out_specs=pl.BlockSpec((XA_ROWS, XA_W), lambda i: (i, 0)),
        out_shape=jax.ShapeDtypeStruct((rows, XA_W), BF16),
        compiler_params=_params(1),
        name="cross_attention",
    )(q, kv)


def _router_kernel(h_ref, w_ref, b_ref, idx_ref, gate_ref, rank_ref, count_ref, run_ref):
    step = pl.program_id(0)
    rows = h_ref.shape[0]

    @pl.when(step == 0)
    def _():
        run_ref[...] = jnp.zeros_like(run_ref)

    logits = jnp.dot(h_ref[...], w_ref[...], preferred_element_type=F32,
                     precision=lax.Precision.HIGHEST) + b_ref[...]
    lane = lax.broadcasted_iota(jnp.int32, (rows, LANES), 1).astype(F32)
    work = jnp.where(lane < N_EXPERTS, logits, -jnp.inf)
    picks, values = [], []
    for _ in range(TOP_K):
        best = jnp.max(work, axis=-1, keepdims=True)
        first = jnp.min(jnp.where(work == best, lane, float(LANES)), axis=-1, keepdims=True)
        hit = lane == first
        picks.append((first, hit))
        values.append(best)
        work = jnp.where(hit, -jnp.inf, work)
    expv = [jnp.exp(v - values[0]) for v in values]
    denom = sum(expv)

    chosen = sum(hit.astype(F32) for _, hit in picks)
    r = lax.broadcasted_iota(jnp.int32, (rows, rows), 0)
    c = lax.broadcasted_iota(jnp.int32, (rows, rows), 1)
    before = (c < r).astype(BF16)
    prefix = jnp.dot(before, chosen.astype(BF16), preferred_element_type=F32) + run_ref[...]

    idx_out = jnp.zeros((rows, LANES), jnp.int32)
    gate_out = jnp.zeros((rows, LANES), F32)
    rank_out = jnp.zeros((rows, LANES), jnp.int32)
    for k, (first, hit) in enumerate(picks):
        rank_k = jnp.sum(jnp.where(hit, prefix, 0.0), axis=-1, keepdims=True).astype(jnp.int32)
        idx_out = jnp.where(lane == k, first.astype(jnp.int32), idx_out)
        gate_out = jnp.where(lane == k, expv[k] / denom, gate_out)
        rank_out = jnp.where(lane == k, rank_k, rank_out)
    idx_ref[...] = idx_out
    gate_ref[...] = gate_out
    rank_ref[...] = rank_out
    run_ref[...] += jnp.sum(chosen, axis=0, keepdims=True)
    count_ref[...] = run_ref[...]


def router_call(h, w_pad, b_pad):
    rows, d = h.shape
    tok = pl.BlockSpec((RT_ROWS, LANES), lambda i: (i, 0))
    return pl.pallas_call(
        _router_kernel,
        grid=(rows // RT_ROWS,),
        in_specs=[pl.BlockSpec((RT_ROWS, d), lambda i: (i, 0)),
                  pl.BlockSpec((d, LANES), lambda i: (0, 0)),
                  pl.BlockSpec((1, LANES), lambda i: (0, 0))],
        out_specs=[tok, tok, tok, pl.BlockSpec((1, LANES), lambda i: (0, 0))],
        out_shape=[jax.ShapeDtypeStruct((rows, LANES), jnp.int32),
                   jax.ShapeDtypeStruct((rows, LANES), F32),
                   jax.ShapeDtypeStruct((rows, LANES), jnp.int32),
                   jax.ShapeDtypeStruct((1, LANES), F32)],
        scratch_shapes=[pltpu.VMEM((1, LANES), F32)],
        compiler_params=_params(1),
        name="router",
    )(h, w_pad, b_pad)


def _slab_copy(zeros_ref, dst_hbm, sem, row):
    return pltpu.make_async_copy(zeros_ref, dst_hbm.at[pl.ds(row, MOE_ROWS)], sem)


def _token_copy(src_ref, dst_ref, sem, src_row, dst_row):
    return pltpu.make_async_copy(src_ref.at[pl.ds(src_row, 1)], dst_ref.at[pl.ds(dst_row, 1)], sem)


def _dispatch_kernel(fill_ref, dest_ref, h_ref, xs_hbm, zeros_ref, slab_sem, row_sem):
    tokens = h_ref.shape[0]

    @pl.when(pl.program_id(0) == 0)
    def _():
        zeros_ref[...] = jnp.zeros_like(zeros_ref)

        def group_slab(e, carry):
            copy = _slab_copy(zeros_ref, xs_hbm, slab_sem, fill_ref[e])
            copy.start()
            copy.wait()
            return carry

        lax.fori_loop(0, N_EXPERTS, group_slab, 0)
        tail0, n_tail = fill_ref[N_EXPERTS], fill_ref[N_EXPERTS + 1]

        def tail_start(s, carry):
            _slab_copy(zeros_ref, xs_hbm, slab_sem, tail0 + s * MOE_ROWS).start()
            return carry

        def tail_wait(s, carry):
            _slab_copy(zeros_ref, xs_hbm, slab_sem, tail0).wait()
            return carry

        lax.fori_loop(0, n_tail, tail_start, 0)
        lax.fori_loop(0, n_tail, tail_wait, 0)

    def start(t, carry):
        for k in range(TOP_K):
            _token_copy(h_ref, xs_hbm, row_sem, t, dest_ref[0, t * TOP_K + k]).start(priority=k % 2)
        return carry

    def wait(t, carry):
        for k in range(TOP_K):
            _token_copy(h_ref, xs_hbm, row_sem, t, 0).wait()
        return carry

    lax.fori_loop(0, tokens, start, 0, unroll=DMA_UNROLL)
    lax.fori_loop(0, tokens, wait, 0, unroll=DMA_UNROLL)


def dispatch_call(fill, dest, slabs, p_rows):
    tokens = slabs.shape[0]
    steps = tokens // DISPATCH_ROWS
    slab = slabs.shape[1:]
    grid_spec = pltpu.PrefetchScalarGridSpec(
        num_scalar_prefetch=1,
        grid=(steps,),
        in_specs=[pl.BlockSpec((None, 1, DISPATCH_ROWS * TOP_K), lambda i, f: (i, 0, 0), memory_space=pltpu.SMEM),
                  pl.BlockSpec((DISPATCH_ROWS,) + slab, lambda i, f: (i, 0, 0))],
        out_specs=pl.BlockSpec(memory_space=pl.ANY),
        scratch_shapes=[pltpu.VMEM((MOE_ROWS,) + slab, slabs.dtype),
                        pltpu.SemaphoreType.DMA(()), pltpu.SemaphoreType.DMA(())],
    )
    return pl.pallas_call(
        _dispatch_kernel,
        grid_spec=grid_spec,
        out_shape=jax.ShapeDtypeStruct((p_rows + MOE_ROWS,) + slab, slabs.dtype),
        compiler_params=_params(1),
        name="moe_dispatch",
    )(fill, dest.reshape(steps, 1, DISPATCH_ROWS * TOP_K), slabs)


def _tile(i, nv_ref):
    return jnp.maximum(jnp.minimum(i, nv_ref[0] - 1), 0)


def _expert_changed(i, te_ref):
    return jnp.logical_or(i == 0, te_ref[i] != te_ref[jnp.maximum(i - 1, 0)])


def _for_real_rows(i, nv_ref, tr_ref, out_ref, compute):
    real = jnp.where(i < nv_ref[0], tr_ref[i], 0)
    sizes = (MOE_ROWS, MOE_ROWS // 2, MOE_ROWS // 4)
    for n, size in enumerate(sizes):
        smaller = sizes[n + 1] if n + 1 < len(sizes) else 0

        @pl.when(jnp.logical_and(real > smaller, real <= size))
        def _(size=size):
            if size < MOE_ROWS:
                out_ref[:MOE_ROWS - size] = jnp.zeros_like(out_ref[:MOE_ROWS - size])
            out_ref[MOE_ROWS - size:] = compute(slice(MOE_ROWS - size, MOE_ROWS))

    @pl.when(real == 0)
    def _():
        out_ref[...] = jnp.zeros_like(out_ref)


def _moe_up_kernel(te_ref, nv_ref, tr_ref, x_ref, wg_ref, wu_ref, bg_ref, bu_ref, a_ref, wgb_ref, wub_ref):
    i = pl.program_id(1)

    @pl.when(_expert_changed(i, te_ref))
    def _():
        wgb_ref[...] = wg_ref[...].astype(BF16)
        wub_ref[...] = wu_ref[...].astype(BF16)

    def compute(rows):
        x = x_ref[rows].reshape(-1, x_ref.shape[1] * x_ref.shape[2])
        g = jnp.dot(x, wgb_ref[...], preferred_element_type=F32) + bg_ref[...]
        u = jnp.dot(x, wub_ref[...], preferred_element_type=F32) + bu_ref[...]
        g = jnp.minimum(g, SWIGLU_LIMIT)
        u = jnp.clip(u, -SWIGLU_LIMIT, SWIGLU_LIMIT)
        return ((u + 1.0) * g * jax.nn.sigmoid(SWIGLU_ALPHA * g)).astype(a_ref.dtype)

    _for_real_rows(i, nv_ref, tr_ref, a_ref, compute)


def moe_up_call(tile_expert, n_valid, tile_rows, xs, w_up, b_up):
    d = xs.shape[1] * xs.shape[2]
    n_tiles = tile_expert.shape[0]
    p_rows = n_tiles * MOE_ROWS
    n_col = D_EXPERT // UP_COLS
    grid_spec = pltpu.PrefetchScalarGridSpec(
        num_scalar_prefetch=3,
        grid=(n_col, n_tiles),
        in_specs=[
            pl.BlockSpec((MOE_ROWS,) + xs.shape[1:], lambda n, i, te, nv, tr: (_tile(i, nv), 0, 0)),
            pl.BlockSpec((None, d, UP_COLS), lambda n, i, te, nv, tr: (te[_tile(i, nv)], 0, n)),
            pl.BlockSpec((None, d, UP_COLS), lambda n, i, te, nv, tr: (te[_tile(i, nv)], 0, n_col + n)),
            pl.BlockSpec((None, 1, UP_COLS), lambda n, i, te, nv, tr: (te[_tile(i, nv)], 0, n)),
            pl.BlockSpec((None, 1, UP_COLS), lambda n, i, te, nv, tr: (te[_tile(i, nv)], 0, n_col + n)),
        ],
        out_specs=pl.BlockSpec((MOE_ROWS, UP_COLS), lambda n, i, te, nv, tr: (i, n)),
        scratch_shapes=[pltpu.VMEM((d, UP_COLS), BF16), pltpu.VMEM((d, UP_COLS), BF16)],
    )
    return pl.pallas_call(
        _moe_up_kernel,
        grid_spec=grid_spec,
        out_shape=jax.ShapeDtypeStruct((p_rows, D_EXPERT), BF16),
        compiler_params=_params(2),
        name="moe_up",
    )(tile_expert, n_valid, tile_rows, xs, w_up, w_up, b_up, b_up)


def _moe_down_kernel(te_ref, nv_ref, tr_ref, a_ref, w_ref, b_ref, y_ref, wb_ref):
    i = pl.program_id(1)

    @pl.when(_expert_changed(i, te_ref))
    def _():
        wb_ref[...] = w_ref[...].astype(BF16)

    def compute(rows):
        y = jnp.dot(a_ref[rows, :], wb_ref[...], preferred_element_type=F32) + b_ref[...]
        return y.reshape((y.shape[0],) + y_ref.shape[1:]).astype(y_ref.dtype)

    _for_real_rows(i, nv_ref, tr_ref, y_ref, compute)


def moe_down_call(tile_expert, n_valid, tile_rows, act, w_down, b_down):
    p_rows = act.shape[0]
    n_tiles = p_rows // MOE_ROWS
    d_out = w_down.shape[2]
    grid_spec = pltpu.PrefetchScalarGridSpec(
        num_scalar_prefetch=3,
        grid=(d_out // DOWN_COLS, n_tiles),
        in_specs=[
            pl.BlockSpec((MOE_ROWS, D_EXPERT), lambda n, i, te, nv, tr: (_tile(i, nv), 0)),
            pl.BlockSpec((None, D_EXPERT, DOWN_COLS), lambda n, i, te, nv, tr: (te[_tile(i, nv)], 0, n)),
            pl.BlockSpec((None, 1, DOWN_COLS), lambda n, i, te, nv, tr: (te[_tile(i, nv)], 0, n)),
        ],
        out_specs=pl.BlockSpec((MOE_ROWS, DOWN_COLS // LANES, LANES), lambda n, i, te, nv, tr: (i, n, 0)),
        scratch_shapes=[pltpu.VMEM((D_EXPERT, DOWN_COLS), BF16)],
    )
    return pl.pallas_call(
        _moe_down_kernel,
        grid_spec=grid_spec,
        out_shape=jax.ShapeDtypeStruct((p_rows, d_out // LANES, LANES), BF16),
        compiler_params=_params(2),
        name="moe_down",
    )(tile_expert, n_valid, tile_rows, act, w_down, b_down)


def _combine_kernel(dest_ref, next_ref, y_hbm, gate_ref, h_ref, g_ref, b_ref, o_ref, buf_ref, sems):
    step = pl.program_id(0)
    rows = h_ref.shape[0]
    slot = step % 2

    def issue(idx_ref, to_slot):
        def start(r, carry):
            for k in range(TOP_K):
                _token_copy(y_hbm, buf_ref.at[to_slot, k], sems.at[to_slot], idx_ref[0, r * TOP_K + k], r
                            ).start(priority=k % 2)
            return carry
        lax.fori_loop(0, rows, start, 0, unroll=DMA_UNROLL)

    @pl.when(step == 0)
    def _():
        issue(dest_ref, 0)

    @pl.when(step + 1 < pl.num_programs(0))
    def _():
        issue(next_ref, 1 - slot)

    def wait(r, carry):
        for k in range(TOP_K):
            _token_copy(y_hbm, buf_ref.at[slot, k], sems.at[slot], 0, r).wait()
        return carry

    lax.fori_loop(0, rows, wait, 0, unroll=DMA_UNROLL)
    ff = None
    for k in range(TOP_K):
        term = gate_ref[:, k:k + 1] * buf_ref[slot, k].astype(F32).reshape(rows, D_MODEL)
        ff = term if ff is None else ff + term
    o_ref[...] = _layer_norm_rows(DEEPNORM_ALPHA * h_ref[...] + ff, g_ref[...], b_ref[...])


def combine_call(dest, y, gate, h, g, b):
    rows, d = h.shape
    steps = rows // COMBINE_ROWS
    dest3 = dest.reshape(steps, 1, COMBINE_ROWS * TOP_K)
    idx_block = (None, 1, COMBINE_ROWS * TOP_K)
    return pl.pallas_call(
        _combine_kernel,
        grid=(steps,),
        in_specs=[pl.BlockSpec(idx_block, lambda i: (i, 0, 0), memory_space=pltpu.SMEM),
                  pl.BlockSpec(idx_block, lambda i: (jnp.minimum(i + 1, steps - 1), 0, 0), memory_space=pltpu.SMEM),
                  pl.BlockSpec(memory_space=pl.ANY),
                  pl.BlockSpec((COMBINE_ROWS, LANES), lambda i: (i, 0)),
                  pl.BlockSpec((COMBINE_ROWS, d), lambda i: (i, 0)),
                  pl.BlockSpec((1, d), lambda i: (0, 0)),
                  pl.BlockSpec((1, d), lambda i: (0, 0))],
        out_specs=pl.BlockSpec((COMBINE_ROWS, d), lambda i: (i, 0)),
        out_shape=jax.ShapeDtypeStruct((rows, d), F32),
        scratch_shapes=[pltpu.VMEM((2, TOP_K, COMBINE_ROWS, ROW_TILES, LANES), y.dtype),
                        pltpu.SemaphoreType.DMA((2,))],
        compiler_params=_params(1),
        name="moe_combine",
    )(dest3, dest3, y, gate, h, g.reshape(1, d), b.reshape(1, d))


def _pad_lanes(a, rows):
    out = jnp.zeros((rows, LANES), a.dtype)
    return out.at[:, :a.shape[-1]].set(a.reshape(rows, -1))


def kernel(x, mem, ln_in_g, ln_in_b, w_in, b_f, fox_q_norm_g, fox_k_norm_g, mix_norm_g, w_out, ln_mix_g, ln_mix_b, mem_ln_g, mem_ln_b, xa_wq, xa_wkv, xa_wo, ln_xa_g, ln_xa_b, router_w, router_b, w_up, b_up, w_down, b_down, ln_moe_g, ln_moe_b):
    batch, seq, d = x.shape
    n_mem = mem.shape[1]
    tokens = batch * seq
    assert w_in.shape[0] == DEPTH == 1 and d == D_MODEL

    h0, h0b = layer_norm_call(x.reshape(tokens, d), ln_in_g, ln_in_b, ("f32", "bf16"))
    w_in_t = jnp.swapaxes(w_in, 1, 2)[0]
    sb0 = 4 * FOX_W + N_FOX_HEADS
    p_fox = matmul_call([(h0b, w_in_t, 0, 0, True)], 4 * FOX_W, BF16, name="in_proj_fox")
    f_logit = matmul_call([(h0b, w_in_t, 0, 4 * FOX_W, True)], LANES, F32, tn=LANES, name="in_proj_forget")
    p_sb = matmul_call([(h0b, w_in_t, 0, sb0, True)], 3 * SB_W, BF16, name="in_proj_sb")

    c = forget_cumsum_call(f_logit, _pad_lanes(b_f[0], 1), batch, seq)
    c = c.reshape(batch, seq, LANES)[:, :, :N_FOX_HEADS].transpose(0, 2, 1)
    gm = mix_norm_g[0].reshape(-1, 1, HEAD_DIM)
    o_f = fox_attention_call(p_fox, c[..., None], c[:, :, None, :],
                             fox_q_norm_g[0].reshape(1, HEAD_DIM), fox_k_norm_g[0].reshape(1, HEAD_DIM),
                             gm, batch, seq)
    o_s = sb_attention_call(p_sb, gm, batch, seq)
    r1 = matmul_call([(o_f, w_out[0], 0, 0, False), (o_s, w_out[0], FOX_W, 0, False)], d, F32,
                     res=h0, alpha=DEEPNORM_ALPHA, name="out_proj")
    h1, h1b = layer_norm_call(r1, ln_mix_g[0], ln_mix_b[0], ("f32", "bf16"))

    (mem_nb,) = layer_norm_call(mem.reshape(batch * n_mem, d), mem_ln_g[0], mem_ln_b[0], ("bf16",))
    kv = matmul_call([(mem_nb, xa_wkv[0], 0, 0, False)], 2 * XA_W, BF16, name="xa_kv_proj")
    qx = matmul_call([(h1b, xa_wq[0], 0, 0, False)], XA_W, BF16, name="xa_q_proj")
    ox = cross_attention_call(qx, kv, seq, n_mem)
    r2 = matmul_call([(ox, xa_wo[0], 0, 0, False)], d, F32, res=h1, alpha=DEEPNORM_ALPHA, name="xa_out_proj")
    h2, h2s = layer_norm_call(r2, ln_xa_g[0], ln_xa_b[0], ("f32", "slab"))

    idx, gate, rank, counts = router_call(h2, _pad_lanes(router_w[0], d), _pad_lanes(router_b[0], 1))
    counts = counts[0, :N_EXPERTS].astype(jnp.int32)
    padded = (counts + MOE_ROWS - 1) // MOE_ROWS * MOE_ROWS
    ends = jnp.cumsum(padded)
    starts = ends - padded
    first = ends - counts
    dest = (first[idx[:, :TOP_K]] + rank[:, :TOP_K]).reshape(-1)
    p_rows = tokens * TOP_K + N_EXPERTS * MOE_ROWS
    n_tiles = p_rows // MOE_ROWS
    tile_row0 = jnp.arange(n_tiles, dtype=jnp.int32) * MOE_ROWS
    tile_expert = jnp.minimum(jnp.sum(ends[None, :] <= tile_row0[:, None], axis=1), N_EXPERTS - 1).astype(jnp.int32)
    tile_rows = jnp.clip(tile_row0 + MOE_ROWS - first[tile_expert], 0, MOE_ROWS).astype(jnp.int32)
    n_valid = (ends[-1:] // MOE_ROWS).astype(jnp.int32)
    fill = jnp.concatenate([starts, ends[-1:], (p_rows + MOE_ROWS - ends[-1:]) // MOE_ROWS]).astype(jnp.int32)

    xs = dispatch_call(fill, dest, h2s, p_rows)
    act = moe_up_call(tile_expert, n_valid, tile_rows, xs, w_up[0], b_up[0].reshape(N_EXPERTS, 1, -1))
    y = moe_down_call(tile_expert, n_valid, tile_rows, act, w_down[0], b_down[0].reshape(N_EXPERTS, 1, -1))
    out = combine_call(dest, y, gate, h2, ln_moe_g[0], ln_moe_b[0])
    return out.reshape(batch, seq, d)
```

```python
import functools

import jax
import jax.numpy as jnp
from jax import lax
from jax.experimental import pallas as pl
from jax.experimental.pallas import tpu as pltpu

F32 = jnp.float32
BF16 = jnp.bfloat16

D_MODEL = 4096
HEAD_DIM = 128
N_FOX_HEADS = 16
N_SB_HEADS = 16
FOX_W = N_FOX_HEADS * HEAD_DIM
SB_W = N_SB_HEADS * HEAD_DIM
XA_HEADS = 4
XA_W = XA_HEADS * HEAD_DIM
N_EXPERTS = 32
TOP_K = 4
D_EXPERT = D_MODEL // 4
SWIGLU_LIMIT = 7.0
SWIGLU_ALPHA = 1.702
LN_EPS = 1e-5
RMS_EPS = 1e-6
DEPTH = 1
DEEPNORM_ALPHA = (2 * DEPTH) ** 0.25
ATTN_SCALE = HEAD_DIM ** -0.5

LANES = 128
BF16_SUBLANES = 16
V7X_VMEM_BYTES = 64 * 1024 * 1024
VMEM_LIMIT = V7X_VMEM_BYTES - 8 * 1024 * 1024

LN_ROWS = 256
MM_ROWS = 2048
MM_SUB = 1024
MM_COLS = 256
ATT_ROWS = 2048
ATT_KEYS = 256
ATT_HEADS = 2
XA_ROWS = 512
RT_ROWS = 256
MOE_ROWS = 512
UP_COLS = 512
DOWN_COLS = 2048
DISPATCH_ROWS = 512
COMBINE_ROWS = 128
DMA_UNROLL = 8

ROW_TILES = D_MODEL // LANES


def _params(n_axes, vmem=VMEM_LIMIT):
    return pltpu.CompilerParams(dimension_semantics=("arbitrary",) * n_axes,
                                vmem_limit_bytes=vmem)


def _layer_norm_rows(xf, g, b):
    mu = jnp.mean(xf, axis=-1, keepdims=True)
    xc = xf - mu
    var = jnp.mean(xc * xc, axis=-1, keepdims=True)
    return xc * lax.rsqrt(var + LN_EPS) * g + b


def _ln_kernel(x_ref, g_ref, b_ref, *out_refs, kinds):
    y = _layer_norm_rows(x_ref[...].astype(F32), g_ref[...], b_ref[...])
    for ref, kind in zip(out_refs, kinds):
        if kind == "f32":
            ref[...] = y
        elif kind == "bf16":
            ref[...] = y.astype(BF16)
        else:
            ref[...] = y.reshape(ref.shape).astype(BF16)


def layer_norm_call(x, g, b, kinds):
    rows, d = x.shape
    out_shape, out_specs = [], []
    for kind in kinds:
        if kind == "slab":
            out_shape.append(jax.ShapeDtypeStruct((rows, ROW_TILES, LANES), BF16))
            out_specs.append(pl.BlockSpec((LN_ROWS, ROW_TILES, LANES), lambda i: (i, 0, 0)))
        else:
            out_shape.append(jax.ShapeDtypeStruct((rows, d), F32 if kind == "f32" else BF16))
            out_specs.append(pl.BlockSpec((LN_ROWS, d), lambda i: (i, 0)))
    return pl.pallas_call(
        functools.partial(_ln_kernel, kinds=kinds),
        grid=(rows // LN_ROWS,),
        in_specs=[pl.BlockSpec((LN_ROWS, d), lambda i: (i, 0)),
                  pl.BlockSpec((1, d), lambda i: (0, 0)),
                  pl.BlockSpec((1, d), lambda i: (0, 0))],
        out_specs=out_specs,
        out_shape=out_shape,
        compiler_params=_params(1),
        name="layer_norm",
    )(x, g.reshape(1, d), b.reshape(1, d))


_NT_DIMS = (((1,), (1,)), ((), ()))


def _mm_kernel(*refs, transposed, has_res, alpha, sub):
    n_pairs = len(transposed)
    lhs_refs = refs[:n_pairs]
    w_refs = refs[n_pairs:2 * n_pairs]
    pos = 2 * n_pairs
    res_ref = refs[pos] if has_res else None
    pos += int(has_res)
    out_ref = refs[pos]
    wb_refs = refs[pos + 1:]
    for w_ref, wb_ref in zip(w_refs, wb_refs):
        wb_ref[...] = w_ref[...].astype(BF16)
    rows = out_ref.shape[0]

    def body(r, carry):
        sl = pl.ds(pl.multiple_of(r * sub, sub), sub)
        acc = None
        for lhs_ref, wb_ref, tr in zip(lhs_refs, wb_refs, transposed):
            if tr:
                part = lax.dot_general(lhs_ref[sl, :], wb_ref[...], _NT_DIMS, preferred_element_type=F32)
            else:
                part = jnp.dot(lhs_ref[sl, :], wb_ref[...], preferred_element_type=F32)
            acc = part if acc is None else acc + part
        if has_res:
            acc = acc + alpha * res_ref[sl, :]
        out_ref[sl, :] = acc.astype(out_ref.dtype)
        return carry

    lax.fori_loop(0, rows // sub, body, 0)


def matmul_call(pairs, n_cols, out_dtype, res=None, alpha=1.0, tn=MM_COLS, tm=MM_ROWS, name="matmul"):
    m = pairs[0][0].shape[0]
    tm = min(tm, m)
    sub = min(MM_SUB, tm)
    in_specs, args, scratch = [], [], []
    for lhs, _, _, _, _ in pairs:
        in_specs.append(pl.BlockSpec((tm, lhs.shape[1]), lambda i, j: (i, 0)))
        args.append(lhs)
    for lhs, w, k0, col0, tr in pairs:
        k = lhs.shape[1]
        if tr:
            in_specs.append(pl.BlockSpec(
                (pl.Element(tn), pl.Element(k)),
                lambda i, j, k0=k0, col0=col0: (pl.multiple_of(col0 + j * tn, BF16_SUBLANES), k0)))
            scratch.append(pltpu.VMEM((tn, k), BF16))
        else:
            in_specs.append(pl.BlockSpec((k, tn), lambda i, j, rb=k0 // k, cb=col0 // tn: (rb, j + cb)))
            scratch.append(pltpu.VMEM((k, tn), BF16))
        args.append(w)
    if res is not None:
        in_specs.append(pl.BlockSpec((tm, tn), lambda i, j: (i, j)))
        args.append(res)
    return pl.pallas_call(
        functools.partial(_mm_kernel, transposed=tuple(p[4] for p in pairs), has_res=res is not None,
                          alpha=alpha, sub=sub),
        grid=(m // tm, n_cols // tn),
        in_specs=in_specs,
        out_specs=pl.BlockSpec((tm, tn), lambda i, j: (i, j)),
        out_shape=jax.ShapeDtypeStruct((m, n_cols), out_dtype),
        scratch_shapes=scratch,
        compiler_params=_params(2),
        name=name,
    )(*args)


def _log_sigmoid(x):
    return jnp.minimum(x, 0.0) - jnp.log1p(jnp.exp(-jnp.abs(x)))


def _split3_bf16(x):
    a = x.astype(BF16)
    r = x - a.astype(F32)
    b = r.astype(BF16)
    c = (r - b.astype(F32)).astype(BF16)
    return a, b, c


def _forget_cumsum_kernel(f_ref, b_ref, c_ref, *, chunk):
    seq = f_ref.shape[0]
    row = lax.broadcasted_iota(jnp.int32, (chunk, chunk), 0)
    col = lax.broadcasted_iota(jnp.int32, (chunk, chunk), 1)
    tri = (col <= row).astype(BF16)
    carry = jnp.zeros((1, f_ref.shape[1]), F32)
    for c0 in range(0, seq, chunk):
        lf = _log_sigmoid(f_ref[c0:c0 + chunk, :] + b_ref[...])
        part = sum(jnp.dot(tri, piece, preferred_element_type=F32) for piece in _split3_bf16(lf))
        cs = part + carry
        c_ref[c0:c0 + chunk, :] = cs
        carry = cs[chunk - 1:chunk, :]


def forget_cumsum_call(f_logit, b_f, batch, seq):
    return pl.pallas_call(
        functools.partial(_forget_cumsum_kernel, chunk=ATT_KEYS),
        grid=(batch,),
        in_specs=[pl.BlockSpec((seq, LANES), lambda b: (b, 0)),
                  pl.BlockSpec((1, LANES), lambda b: (0, 0))],
        out_specs=pl.BlockSpec((seq, LANES), lambda b: (b, 0)),
        out_shape=jax.ShapeDtypeStruct((batch * seq, LANES), F32),
        compiler_params=_params(1),
        name="forget_cumsum",
    )(f_logit, b_f)


def _rms_rows(xf, g):
    return xf * lax.rsqrt(jnp.mean(xf * xf, axis=-1, keepdims=True) + RMS_EPS) * g


def _head_cols(a):
    return slice(a * HEAD_DIM, (a + 1) * HEAD_DIM)


def _store_values_transposed(v_ref, vt_ref):
    for a in range(ATT_HEADS):
        for s0 in range(0, v_ref.shape[0], ATT_KEYS):
            vt_ref[a, :, s0:s0 + ATT_KEYS] = v_ref[s0:s0 + ATT_KEYS, _head_cols(a)].astype(F32).T.astype(BF16)


def _fox_kernel(q_ref, k_ref, v_ref, gate_ref, ckey_ref, cqry_ref, gq_ref, gk_ref, gm_ref,
                o_ref, kn_ref, vt_ref, m_ref, l_ref, acc_ref):
    qi = pl.program_id(2)
    seq = k_ref.shape[0]
    nq = q_ref.shape[0]
    blk = ATT_KEYS
    per = nq // blk
    heads = range(ATT_HEADS)

    @pl.when(qi == 0)
    def _():
        _store_values_transposed(v_ref, vt_ref)
        for a in heads:
            for s0 in range(0, seq, blk):
                kn = _rms_rows(k_ref[s0:s0 + blk, _head_cols(a)].astype(F32), gk_ref[...])
                kn_ref[a, s0:s0 + blk, :] = kn.astype(BF16)

    qt = [(_rms_rows(q_ref[:, _head_cols(a)].astype(F32), gq_ref[...]) * ATTN_SCALE).T.astype(BF16)
          for a in heads]
    c_q = [cqry_ref[a] for a in heads]

    def step(a, j, col0, diagonal):
        cols = slice(col0, nq)
        m_prev = m_ref[a, :, cols]
        ks = pl.ds(pl.multiple_of(j * blk, blk), blk)
        s = jnp.dot(kn_ref[a, ks, :], qt[a][:, cols], preferred_element_type=F32)
        s = s + (c_q[a][:, cols] - ckey_ref[a, ks, :])
        if diagonal:
            key = lax.broadcasted_iota(jnp.int32, s.shape, 0)
            qry = lax.broadcasted_iota(jnp.int32, s.shape, 1)
            s = jnp.where(key <= qry, s, -jnp.inf)
        m_new = jnp.maximum(m_prev, jnp.max(s, axis=0, keepdims=True))
        p = jnp.exp(s - m_new)
        scale = jnp.exp(m_prev - m_new)
        m_ref[a, :, cols] = m_new
        l_ref[a, :, cols] = scale * l_ref[a, :, cols] + jnp.sum(p, axis=0, keepdims=True)
        acc_ref[a, :, cols] = scale * acc_ref[a, :, cols] + jnp.dot(
            vt_ref[a, :, ks], p.astype(BF16), preferred_element_type=F32)

    m_ref[...] = jnp.full(m_ref.shape, -jnp.inf, F32)
    l_ref[...] = jnp.zeros_like(l_ref)
    acc_ref[...] = jnp.zeros_like(acc_ref)

    def full_blocks(it, carry):
        for r in range(per):
            for a in heads:
                step(a, per * it + r, 0, False)
        return carry

    lax.fori_loop(0, qi, full_blocks, 0)
    for r in range(per):
        for a in heads:
            step(a, per * qi + r, r * blk, True)
    for a in heads:
        o = _rms_rows((acc_ref[a] / l_ref[a]).T, gm_ref[a]) * jax.nn.sigmoid(gate_ref[:, _head_cols(a)].astype(F32))
        o_ref[:, _head_cols(a)] = o.astype(o_ref.dtype)


def fox_attention_call(p_fox, c_col, c_row, gq, gk, gm, batch, seq):
    nq = seq // ATT_ROWS
    width = ATT_HEADS * HEAD_DIM
    groups = N_FOX_HEADS // ATT_HEADS
    return pl.pallas_call(
        _fox_kernel,
        grid=(batch, groups, nq),
        in_specs=[
            pl.BlockSpec((ATT_ROWS, width), lambda b, hp, qi: (b * nq + qi, hp)),
            pl.BlockSpec((seq, width), lambda b, hp, qi: (b, groups + hp)),
            pl.BlockSpec((seq, width), lambda b, hp, qi: (b, 2 * groups + hp)),
            pl.BlockSpec((ATT_ROWS, width), lambda b, hp, qi: (b * nq + qi, 3 * groups + hp)),
            pl.BlockSpec((None, ATT_HEADS, seq, 1), lambda b, hp, qi: (b, hp, 0, 0)),
            pl.BlockSpec((None, ATT_HEADS, 1, ATT_ROWS), lambda b, hp, qi: (b, hp, 0, qi)),
            pl.BlockSpec((1, HEAD_DIM), lambda b, hp, qi: (0, 0)),
            pl.BlockSpec((1, HEAD_DIM), lambda b, hp, qi: (0, 0)),
            pl.BlockSpec((ATT_HEADS, 1, HEAD_DIM), lambda b, hp, qi: (hp, 0, 0)),
        ],
        out_specs=pl.BlockSpec((ATT_ROWS, width), lambda b, hp, qi: (b * nq + qi, hp)),
        out_shape=jax.ShapeDtypeStruct((batch * seq, FOX_W), BF16),
        scratch_shapes=[pltpu.VMEM((ATT_HEADS, seq, HEAD_DIM), BF16), pltpu.VMEM((ATT_HEADS, HEAD_DIM, seq), BF16),
                        pltpu.VMEM((ATT_HEADS, 1, ATT_ROWS), F32), pltpu.VMEM((ATT_HEADS, 1, ATT_ROWS), F32),
                        pltpu.VMEM((ATT_HEADS, HEAD_DIM, ATT_ROWS), F32)],
        compiler_params=_params(3),
        name="fox_attention",
    )(p_fox, p_fox, p_fox, p_fox, c_col, c_row, gq, gk, gm)


def _sb_kernel(q_ref, k_ref, v_ref, gm_ref, o_ref, vt_ref, after_ref, later_ref, acc_ref):
    qi = pl.program_id(2)
    nq = q_ref.shape[0]
    blk = ATT_KEYS
    per = nq // blk
    heads = range(ATT_HEADS)

    @pl.when(qi == 0)
    def _():
        _store_values_transposed(v_ref, vt_ref)
        row = lax.broadcasted_iota(jnp.int32, (blk, blk), 0)
        col = lax.broadcasted_iota(jnp.int32, (blk, blk), 1)
        after_ref[...] = (col > row).astype(BF16)

    qt = [(q_ref[:, _head_cols(a)].astype(F32) * ATTN_SCALE).T.astype(BF16) for a in heads]

    def step(a, j, col0, diagonal):
        cols = slice(col0, nq)
        ks = pl.ds(pl.multiple_of(j * blk, blk), blk)
        z = jnp.dot(k_ref[ks, _head_cols(a)], qt[a][:, cols], preferred_element_type=F32)
        softplus = jnp.maximum(z, 0.0) + jnp.log(1.0 + jnp.exp(-jnp.abs(z)))
        log_1m = -softplus
        if diagonal:
            key = lax.broadcasted_iota(jnp.int32, z.shape, 0)
            qry = lax.broadcasted_iota(jnp.int32, z.shape, 1)
            strict = key < qry
            log_1m = jnp.where(strict, log_1m, 0.0)
        hi = log_1m.astype(BF16)
        lo = (log_1m - hi.astype(F32)).astype(BF16)
        later = later_ref[a, :, cols]
        tail = (jnp.dot(after_ref[...], hi, preferred_element_type=F32)
                + jnp.dot(after_ref[...], lo, preferred_element_type=F32)) + later
        w = jnp.exp((z - softplus) + tail)
        if diagonal:
            w = jnp.where(strict, w, 0.0)
        acc_ref[a, :, cols] += jnp.dot(vt_ref[a, :, ks], w.astype(BF16), preferred_element_type=F32)
        later_ref[a, :, cols] = later + jnp.sum(log_1m, axis=0, keepdims=True)

    later_ref[...] = jnp.zeros_like(later_ref)
    acc_ref[...] = jnp.zeros_like(acc_ref)
    for r in reversed(range(per)):
        for a in heads:
            step(a, per * qi + r, r * blk, True)

    def full_blocks(it, carry):
        for r in range(per):
            for a in heads:
                step(a, per * (qi - it) - 1 - r, 0, False)
        return carry

    lax.fori_loop(0, qi, full_blocks, 0)
    for a in heads:
        o_ref[:, _head_cols(a)] = _rms_rows(acc_ref[a].T, gm_ref[a]).astype(o_ref.dtype)


def sb_attention_call(p_sb, gm, batch, seq):
    nq = seq // ATT_ROWS
    width = ATT_HEADS * HEAD_DIM
    groups = N_SB_HEADS // ATT_HEADS
    return pl.pallas_call(
        _sb_kernel,
        grid=(batch, groups, nq),
        in_specs=[
            pl.BlockSpec((ATT_ROWS, width), lambda b, hp, qi: (b * nq + qi, hp)),
            pl.BlockSpec((seq, width), lambda b, hp, qi: (b, groups + hp)),
            pl.BlockSpec((seq, width), lambda b, hp, qi: (b, 2 * groups + hp)),
            pl.BlockSpec((ATT_HEADS, 1, HEAD_DIM), lambda b, hp, qi: (N_FOX_HEADS // ATT_HEADS + hp, 0, 0)),
        ],
        out_specs=pl.BlockSpec((ATT_ROWS, width), lambda b, hp, qi: (b * nq + qi, hp)),
        out_shape=jax.ShapeDtypeStruct((batch * seq, SB_W), BF16),
        scratch_shapes=[pltpu.VMEM((ATT_HEADS, HEAD_DIM, seq), BF16), pltpu.VMEM((ATT_KEYS, ATT_KEYS), BF16),
                        pltpu.VMEM((ATT_HEADS, 1, ATT_ROWS), F32), pltpu.VMEM((ATT_HEADS, HEAD_DIM, ATT_ROWS), F32)],
        compiler_params=_params(3),
        name="sb_attention",
    )(p_sb, p_sb, p_sb, gm)


def _xattn_kernel(q_ref, kv_ref, o_ref):
    for hd in range(XA_HEADS):
        cols = _head_cols(hd)
        q = q_ref[:, cols]
        k = kv_ref[:, cols]
        v = kv_ref[:, XA_W + hd * HEAD_DIM:XA_W + (hd + 1) * HEAD_DIM]
        s = lax.dot_general(q, k, _NT_DIMS, preferred_element_type=F32) * ATTN_SCALE
        p = jnp.exp(s - jnp.max(s, axis=-1, keepdims=True))
        p = p / jnp.sum(p, axis=-1, keepdims=True)
        o_ref[:, cols] = jnp.dot(p.astype(BF16), v, preferred_element_type=F32).astype(o_ref.dtype)


def cross_attention_call(q, kv, seq, n_mem):
    rows = q.shape[0]
    per_batch = seq // XA_ROWS
    return pl.pallas_call(
        _xattn_kernel,
        grid=(rows // XA_ROWS,),
        in_specs=[pl.BlockSpec((XA_ROWS, XA_W), lambda i: (i, 0)),
                  pl.BlockSpec((n_mem, 2 * XA_W), lambda i: (i // per_batch, 0))],
        out_specs=pl.BlockSpec((XA_ROWS, XA_W), lambda i: (i, 0)),
        out_shape=jax.ShapeDtypeStruct((rows, XA_W), BF16),
        compiler_params=_params(1),
        name="cross_attention",
    )(q, kv)


def _router_kernel(h_ref, w_ref, b_ref, idx_ref, gate_ref, rank_ref, count_ref, run_ref):
    step = pl.program_id(0)
    rows = h_ref.shape[0]

    @pl.when(step == 0)
    def _():
        run_ref[...] = jnp.zeros_like(run_ref)

    h_hi, h_lo, _ = _split3_bf16(h_ref[...])
    w_hi, w_lo, _ = _split3_bf16(w_ref[...])
    logits = (jnp.dot(h_hi, w_hi, preferred_element_type=F32)
              + (jnp.dot(h_lo, w_hi, preferred_element_type=F32)
                 + jnp.dot(h_hi, w_lo, preferred_element_type=F32))) + b_ref[...]
    lane = lax.broadcasted_iota(jnp.int32, (rows, LANES), 1).astype(F32)
    work = jnp.where(lane < N_EXPERTS, logits, -jnp.inf)
    picks, values = [], []
    for _ in range(TOP_K):
        best = jnp.max(work, axis=-1, keepdims=True)
        first = jnp.min(jnp.where(work == best, lane, float(LANES)), axis=-1, keepdims=True)
        hit = lane == first
        picks.append((first, hit))
        values.append(best)
        work = jnp.where(hit, -jnp.inf, work)
    expv = [jnp.exp(v - values[0]) for v in values]
    denom = sum(expv)

    chosen = sum(hit.astype(F32) for _, hit in picks)
    r = lax.broadcasted_iota(jnp.int32, (rows, rows), 0)
    c = lax.broadcasted_iota(jnp.int32, (rows, rows), 1)
    before = (c < r).astype(BF16)
    prefix = jnp.dot(before, chosen.astype(BF16), preferred_element_type=F32) + run_ref[...]

    idx_out = jnp.zeros((rows, LANES), jnp.int32)
    gate_out = jnp.zeros((rows, LANES), F32)
    rank_out = jnp.zeros((rows, LANES), jnp.int32)
    for k, (first, hit) in enumerate(picks):
        rank_k = jnp.sum(jnp.where(hit, prefix, 0.0), axis=-1, keepdims=True).astype(jnp.int32)
        idx_out = jnp.where(lane == k, first.astype(jnp.int32), idx_out)
        gate_out = jnp.where(lane == k, expv[k] / denom, gate_out)
        rank_out = jnp.where(lane == k, rank_k, rank_out)
    idx_ref[...] = idx_out
    gate_ref[...] = gate_out
    rank_ref[...] = rank_out
    run_ref[...] += jnp.sum(chosen, axis=0, keepdims=True)
    count_ref[...] = run_ref[...]


def router_call(h, w_pad, b_pad):
    rows, d = h.shape
    tok = pl.BlockSpec((RT_ROWS, LANES), lambda i: (i, 0))
    return pl.pallas_call(
        _router_kernel,
        grid=(rows // RT_ROWS,),
        in_specs=[pl.BlockSpec((RT_ROWS, d), lambda i: (i, 0)),
                  pl.BlockSpec((d, LANES), lambda i: (0, 0)),
                  pl.BlockSpec((1, LANES), lambda i: (0, 0))],
        out_specs=[tok, tok, tok, pl.BlockSpec((1, LANES), lambda i: (0, 0))],
        out_shape=[jax.ShapeDtypeStruct((rows, LANES), jnp.int32),
                   jax.ShapeDtypeStruct((rows, LANES), F32),
                   jax.ShapeDtypeStruct((rows, LANES), jnp.int32),
                   jax.ShapeDtypeStruct((1, LANES), F32)],
        scratch_shapes=[pltpu.VMEM((1, LANES), F32)],
        compiler_params=_params(1),
        name="router",
    )(h, w_pad, b_pad)


def _slab_copy(zeros_ref, dst_hbm, sem, row):
    return pltpu.make_async_copy(zeros_ref, dst_hbm.at[pl.ds(row, MOE_ROWS)], sem)


def _token_copy(src_ref, dst_ref, sem, src_row, dst_row):
    return pltpu.make_async_copy(src_ref.at[pl.ds(src_row, 1)], dst_ref.at[pl.ds(dst_row, 1)], sem)


def _dispatch_kernel(fill_ref, dest_ref, h_ref, xs_hbm, zeros_ref, slab_sem, row_sem):
    tokens = h_ref.shape[0]

    @pl.when(pl.program_id(0) == 0)
    def _():
        zeros_ref[...] = jnp.zeros_like(zeros_ref)

        def group_slab(e, carry):
            copy = _slab_copy(zeros_ref, xs_hbm, slab_sem, fill_ref[e])
            copy.start()
            copy.wait()
            return carry

        lax.fori_loop(0, N_EXPERTS, group_slab, 0)
        tail0, n_tail = fill_ref[N_EXPERTS], fill_ref[N_EXPERTS + 1]

        def tail_start(s, carry):
            _slab_copy(zeros_ref, xs_hbm, slab_sem, tail0 + s * MOE_ROWS).start()
            return carry

        def tail_wait(s, carry):
            _slab_copy(zeros_ref, xs_hbm, slab_sem, tail0).wait()
            return carry

        lax.fori_loop(0, n_tail, tail_start, 0)
        lax.fori_loop(0, n_tail, tail_wait, 0)

    def start(t, carry):
        for k in range(TOP_K):
            _token_copy(h_ref, xs_hbm, row_sem, t, dest_ref[0, t * TOP_K + k]).start(priority=k % 2)
        return carry

    def wait(t, carry):
        for k in range(TOP_K):
            _token_copy(h_ref, xs_hbm, row_sem, t, 0).wait()
        return carry

    lax.fori_loop(0, tokens, start, 0, unroll=DMA_UNROLL)
    lax.fori_loop(0, tokens, wait, 0, unroll=DMA_UNROLL)


def dispatch_call(fill, dest, slabs, p_rows):
    tokens = slabs.shape[0]
    steps = tokens // DISPATCH_ROWS
    slab = slabs.shape[1:]
    grid_spec = pltpu.PrefetchScalarGridSpec(
        num_scalar_prefetch=1,
        grid=(steps,),
        in_specs=[pl.BlockSpec((None, 1, DISPATCH_ROWS * TOP_K), lambda i, f: (i, 0, 0), memory_space=pltpu.SMEM),
                  pl.BlockSpec((DISPATCH_ROWS,) + slab, lambda i, f: (i, 0, 0))],
        out_specs=pl.BlockSpec(memory_space=pl.ANY),
        scratch_shapes=[pltpu.VMEM((MOE_ROWS,) + slab, slabs.dtype),
                        pltpu.SemaphoreType.DMA(()), pltpu.SemaphoreType.DMA(())],
    )
    return pl.pallas_call(
        _dispatch_kernel,
        grid_spec=grid_spec,
        out_shape=jax.ShapeDtypeStruct((p_rows + MOE_ROWS,) + slab, slabs.dtype),
        compiler_params=_params(1),
        name="moe_dispatch",
    )(fill, dest.reshape(steps, 1, DISPATCH_ROWS * TOP_K), slabs)


def _tile(i, nv_ref):
    return jnp.maximum(jnp.minimum(i, nv_ref[0] - 1), 0)


def _expert_changed(i, te_ref):
    return jnp.logical_or(i == 0, te_ref[i] != te_ref[jnp.maximum(i - 1, 0)])


def _for_real_rows(i, nv_ref, tr_ref, out_ref, compute):
    real = jnp.where(i < nv_ref[0], tr_ref[i], 0)
    sizes = (MOE_ROWS, MOE_ROWS // 2, MOE_ROWS // 4)
    for n, size in enumerate(sizes):
        smaller = sizes[n + 1] if n + 1 < len(sizes) else 0

        @pl.when(jnp.logical_and(real > smaller, real <= size))
        def _(size=size):
            if size < MOE_ROWS:
                out_ref[:MOE_ROWS - size] = jnp.zeros_like(out_ref[:MOE_ROWS - size])
            out_ref[MOE_ROWS - size:] = compute(slice(MOE_ROWS - size, MOE_ROWS))

    @pl.when(real == 0)
    def _():
        out_ref[...] = jnp.zeros_like(out_ref)


def _moe_up_kernel(te_ref, nv_ref, tr_ref, x_ref, wg_ref, wu_ref, bg_ref, bu_ref, a_ref, wgb_ref, wub_ref):
    i = pl.program_id(1)

    @pl.when(_expert_changed(i, te_ref))
    def _():
        wgb_ref[...] = wg_ref[...].astype(BF16)
        wub_ref[...] = wu_ref[...].astype(BF16)

    def compute(rows):
        x = x_ref[rows].reshape(-1, x_ref.shape[1] * x_ref.shape[2])
        g = jnp.dot(x, wgb_ref[...], preferred_element_type=F32) + bg_ref[...]
        u = jnp.dot(x, wub_ref[...], preferred_element_type=F32) + bu_ref[...]
        g = jnp.minimum(g, SWIGLU_LIMIT)
        u = jnp.clip(u, -SWIGLU_LIMIT, SWIGLU_LIMIT)
        return ((u + 1.0) * g * jax.nn.sigmoid(SWIGLU_ALPHA * g)).astype(a_ref.dtype)

    _for_real_rows(i, nv_ref, tr_ref, a_ref, compute)


def moe_up_call(tile_expert, n_valid, tile_rows, xs, w_up, b_up):
    d = xs.shape[1] * xs.shape[2]
    n_tiles = tile_expert.shape[0]
    p_rows = n_tiles * MOE_ROWS
    n_col = D_EXPERT // UP_COLS
    grid_spec = pltpu.PrefetchScalarGridSpec(
        num_scalar_prefetch=3,
        grid=(n_col, n_tiles),
        in_specs=[
            pl.BlockSpec((MOE_ROWS,) + xs.shape[1:], lambda n, i, te, nv, tr: (_tile(i, nv), 0, 0)),
            pl.BlockSpec((None, d, UP_COLS), lambda n, i, te, nv, tr: (te[_tile(i, nv)], 0, n)),
            pl.BlockSpec((None, d, UP_COLS), lambda n, i, te, nv, tr: (te[_tile(i, nv)], 0, n_col + n)),
            pl.BlockSpec((None, 1, UP_COLS), lambda n, i, te, nv, tr: (te[_tile(i, nv)], 0, n)),
            pl.BlockSpec((None, 1, UP_COLS), lambda n, i, te, nv, tr: (te[_tile(i, nv)], 0, n_col + n)),
        ],
        out_specs=pl.BlockSpec((MOE_ROWS, UP_COLS), lambda n, i, te, nv, tr: (i, n)),
        scratch_shapes=[pltpu.VMEM((d, UP_COLS), BF16), pltpu.VMEM((d, UP_COLS), BF16)],
    )
    return pl.pallas_call(
        _moe_up_kernel,
        grid_spec=grid_spec,
        out_shape=jax.ShapeDtypeStruct((p_rows, D_EXPERT), BF16),
        compiler_params=_params(2),
        name="moe_up",
    )(tile_expert, n_valid, tile_rows, xs, w_up, w_up, b_up, b_up)


def _moe_down_kernel(te_ref, nv_ref, tr_ref, a_ref, w_ref, b_ref, y_ref, wb_ref):
    i = pl.program_id(1)

    @pl.when(_expert_changed(i, te_ref))
    def _():
        wb_ref[...] = w_ref[...].astype(BF16)

    def compute(rows):
        y = jnp.dot(a_ref[rows, :], wb_ref[...], preferred_element_type=F32) + b_ref[...]
        return y.reshape((y.shape[0],) + y_ref.shape[1:]).astype(y_ref.dtype)

    _for_real_rows(i, nv_ref, tr_ref, y_ref, compute)


def moe_down_call(tile_expert, n_valid, tile_rows, act, w_down, b_down):
    p_rows = act.shape[0]
    n_tiles = p_rows // MOE_ROWS
    d_out = w_down.shape[2]
    grid_spec = pltpu.PrefetchScalarGridSpec(
        num_scalar_prefetch=3,
        grid=(d_out // DOWN_COLS, n_tiles),
        in_specs=[
            pl.BlockSpec((MOE_ROWS, D_EXPERT), lambda n, i, te, nv, tr: (_tile(i, nv), 0)),
            pl.BlockSpec((None, D_EXPERT, DOWN_COLS), lambda n, i, te, nv, tr: (te[_tile(i, nv)], 0, n)),
            pl.BlockSpec((None, 1, DOWN_COLS), lambda n, i, te, nv, tr: (te[_tile(i, nv)], 0, n)),
        ],
        out_specs=pl.BlockSpec((MOE_ROWS, DOWN_COLS // LANES, LANES), lambda n, i, te, nv, tr: (i, n, 0)),
        scratch_shapes=[pltpu.VMEM((D_EXPERT, DOWN_COLS), BF16)],
    )
    return pl.pallas_call(
        _moe_down_kernel,
        grid_spec=grid_spec,
        out_shape=jax.ShapeDtypeStruct((p_rows, d_out // LANES, LANES), BF16),
        compiler_params=_params(2),
        name="moe_down",
    )(tile_expert, n_valid, tile_rows, act, w_down, b_down)


def _combine_kernel(dest_ref, next_ref, y_hbm, gate_ref, h_ref, g_ref, b_ref, o_ref, buf_ref, sems):
    step = pl.program_id(0)
    rows = h_ref.shape[0]
    slot = step % 2

    def issue(idx_ref, to_slot):
        def start(r, carry):
            for k in range(TOP_K):
                _token_copy(y_hbm, buf_ref.at[to_slot, k], sems.at[to_slot], idx_ref[0, r * TOP_K + k], r
                            ).start(priority=k % 2)
            return carry
        lax.fori_loop(0, rows, start, 0, unroll=DMA_UNROLL)

    @pl.when(step == 0)
    def _():
        issue(dest_ref, 0)

    @pl.when(step + 1 < pl.num_programs(0))
    def _():
        issue(next_ref, 1 - slot)

    def wait(r, carry):
        for k in range(TOP_K):
            _token_copy(y_hbm, buf_ref.at[slot, k], sems.at[slot], 0, r).wait()
        return carry

    lax.fori_loop(0, rows, wait, 0, unroll=DMA_UNROLL)
    ff = None
    for k in range(TOP_K):
        term = gate_ref[:, k:k + 1] * buf_ref[slot, k].astype(F32).reshape(rows, D_MODEL)
        ff = term if ff is None else ff + term
    o_ref[...] = _layer_norm_rows(DEEPNORM_ALPHA * h_ref[...] + ff, g_ref[...], b_ref[...])


def combine_call(dest, y, gate, h, g, b):
    rows, d = h.shape
    steps = rows // COMBINE_ROWS
    dest3 = dest.reshape(steps, 1, COMBINE_ROWS * TOP_K)
    idx_block = (None, 1, COMBINE_ROWS * TOP_K)
    return pl.pallas_call(
        _combine_kernel,
        grid=(steps,),
        in_specs=[pl.BlockSpec(idx_block, lambda i: (i, 0, 0), memory_space=pltpu.SMEM),
                  pl.BlockSpec(idx_block, lambda i: (jnp.minimum(i + 1, steps - 1), 0, 0), memory_space=pltpu.SMEM),
                  pl.BlockSpec(memory_space=pl.ANY),
                  pl.BlockSpec((COMBINE_ROWS, LANES), lambda i: (i, 0)),
                  pl.BlockSpec((COMBINE_ROWS, d), lambda i: (i, 0)),
                  pl.BlockSpec((1, d), lambda i: (0, 0)),
                  pl.BlockSpec((1, d), lambda i: (0, 0))],
        out_specs=pl.BlockSpec((COMBINE_ROWS, d), lambda i: (i, 0)),
        out_shape=jax.ShapeDtypeStruct((rows, d), F32),
        scratch_shapes=[pltpu.VMEM((2, TOP_K, COMBINE_ROWS, ROW_TILES, LANES), y.dtype),
                        pltpu.SemaphoreType.DMA((2,))],
        compiler_params=_params(1),
        name="moe_combine",
    )(dest3, dest3, y, gate, h, g.reshape(1, d), b.reshape(1, d))


def _pad_lanes(a, rows):
    out = jnp.zeros((rows, LANES), a.dtype)
    return out.at[:, :a.shape[-1]].set(a.reshape(rows, -1))


def kernel(x, mem, ln_in_g, ln_in_b, w_in, b_f, fox_q_norm_g, fox_k_norm_g, mix_norm_g, w_out, ln_mix_g, ln_mix_b, mem_ln_g, mem_ln_b, xa_wq, xa_wkv, xa_wo, ln_xa_g, ln_xa_b, router_w, router_b, w_up, b_up, w_down, b_down, ln_moe_g, ln_moe_b):
    batch, seq, d = x.shape
    n_mem = mem.shape[1]
    tokens = batch * seq
    assert w_in.shape[0] == DEPTH == 1 and d == D_MODEL

    h0, h0b = layer_norm_call(x.reshape(tokens, d), ln_in_g, ln_in_b, ("f32", "bf16"))
    w_in_t = jnp.swapaxes(w_in, 1, 2)[0]
    sb0 = 4 * FOX_W + N_FOX_HEADS
    p_fox = matmul_call([(h0b, w_in_t, 0, 0, True)], 4 * FOX_W, BF16, name="in_proj_fox")
    f_logit = matmul_call([(h0b, w_in_t, 0, 4 * FOX_W, True)], LANES, F32, tn=LANES, name="in_proj_forget")
    p_sb = matmul_call([(h0b, w_in_t, 0, sb0, True)], 3 * SB_W, BF16, name="in_proj_sb")

    c = forget_cumsum_call(f_logit, _pad_lanes(b_f[0], 1), batch, seq)
    c = c.reshape(batch, seq, LANES)[:, :, :N_FOX_HEADS].transpose(0, 2, 1)
    gm = mix_norm_g[0].reshape(-1, 1, HEAD_DIM)
    o_f = fox_attention_call(p_fox, c[..., None], c[:, :, None, :],
                             fox_q_norm_g[0].reshape(1, HEAD_DIM), fox_k_norm_g[0].reshape(1, HEAD_DIM),
                             gm, batch, seq)
    o_s = sb_attention_call(p_sb, gm, batch, seq)
    r1 = matmul_call([(o_f, w_out[0], 0, 0, False), (o_s, w_out[0], FOX_W, 0, False)], d, F32,
                     res=h0, alpha=DEEPNORM_ALPHA, name="out_proj")
    h1, h1b = layer_norm_call(r1, ln_mix_g[0], ln_mix_b[0], ("f32", "bf16"))

    (mem_nb,) = layer_norm_call(mem.reshape(batch * n_mem, d), mem_ln_g[0], mem_ln_b[0], ("bf16",))
    kv = matmul_call([(mem_nb, xa_wkv[0], 0, 0, False)], 2 * XA_W, BF16, name="xa_kv_proj")
    qx = matmul_call([(h1b, xa_wq[0], 0, 0, False)], XA_W, BF16, name="xa_q_proj")
    ox = cross_attention_call(qx, kv, seq, n_mem)
    r2 = matmul_call([(ox, xa_wo[0], 0, 0, False)], d, F32, res=h1, alpha=DEEPNORM_ALPHA, name="xa_out_proj")
    h2, h2s = layer_norm_call(r2, ln_xa_g[0], ln_xa_b[0], ("f32", "slab"))

    idx, gate, rank, counts = router_call(h2, _pad_lanes(router_w[0], d), _pad_lanes(router_b[0], 1))
    counts = counts[0, :N_EXPERTS].astype(jnp.int32)
    padded = (counts + MOE_ROWS - 1) // MOE_ROWS * MOE_ROWS
    ends = jnp.cumsum(padded)
    starts = ends - padded
    first = ends - counts
    dest = (first[idx[:, :TOP_K]] + rank[:, :TOP_K]).reshape(-1)
    p_rows = tokens * TOP_K + N_EXPERTS * MOE_ROWS
    n_tiles = p_rows // MOE_ROWS
    tile_row0 = jnp.arange(n_tiles, dtype=jnp.int32) * MOE_ROWS
    tile_expert = jnp.minimum(jnp.sum(ends[None, :] <= tile_row0[:, None], axis=1), N_EXPERTS - 1).astype(jnp.int32)
    tile_rows = jnp.clip(tile_row0 + MOE_ROWS - first[tile_expert], 0, MOE_ROWS).astype(jnp.int32)
    n_valid = (ends[-1:] // MOE_ROWS).astype(jnp.int32)
    fill = jnp.concatenate([starts, ends[-1:], (p_rows + MOE_ROWS - ends[-1:]) // MOE_ROWS]).astype(jnp.int32)

    xs = dispatch_call(fill, dest, h2s, p_rows)
    act = moe_up_call(tile_expert, n_valid, tile_rows, xs, w_up[0], b_up[0].reshape(N_EXPERTS, 1, -1))
    y = moe_down_call(tile_expert, n_valid, tile_rows, act, w_down[0], b_down[0].reshape(N_EXPERTS, 1, -1))
    out = combine_call(dest, y, gate, h2, ln_moe_g[0], ln_moe_b[0])
    return out.reshape(batch, seq, d)
```
